```python
import jax, jax.numpy as jnp
from jax import lax
import numpy as np

D_MODEL = 1024
BATCH = 1
SEQ = 16384
DEPTH = 1
DEC_BATCH = 32
DEC_SEQ = 8
PAST_LEN = 16384
PAGE_SIZE = 128

FOX_HEADS = 8
FOX_HEAD_DIM = 64
FOX_WIDTH = FOX_HEADS * FOX_HEAD_DIM
FOX_BLOCK_Q = 128
GDN_HEADS = 4
GDN_HEAD_DIM = 128
GDN_WIDTH = GDN_HEADS * GDN_HEAD_DIM
GDN_CONV = 4
GDN_CHUNK = 64
N_GROUPS = 4
EXPERTS_PER_GROUP = 8
N_EXPERTS = N_GROUPS * EXPERTS_PER_GROUP
TOP_K_IN_GROUP = 2
EXPERT_FF = 512
MOE_TOKEN_BLOCK = 2048
DEEPNORM_ALPHA = (2.0 * DEPTH) ** 0.25
DEEPNORM_BETA = (8.0 * DEPTH) ** -0.25
LN_EPS = 1e-5
NORM_EPS = 1e-6

kernel_name = 'fox_gdn_hier_moe_deepnorm_adaln_step'

F32 = jnp.float32


def _layer_norm(x, g, b):
    xf = x.astype(F32)
    mu = jnp.mean(xf, axis=-1, keepdims=True)
    var = jnp.mean(jnp.square(xf - mu), axis=-1, keepdims=True)
    return ((xf - mu) * lax.rsqrt(var + LN_EPS) * g + b).astype(x.dtype)


def _l2norm(x):
    xf = x.astype(F32)
    return xf * lax.rsqrt(jnp.sum(xf * xf, axis=-1, keepdims=True) + NORM_EPS)


def _adaln(c, w, b):
    mod = jnp.einsum('bd,de->be', jax.nn.silu(c), w) + b
    shift, scale, gate = jnp.split(mod, 3, axis=-1)
    return shift[:, None], scale[:, None], gate[:, None]


def _split_in(proj):
    sizes = (FOX_WIDTH, FOX_WIDTH, FOX_WIDTH, FOX_HEADS,
             GDN_WIDTH, GDN_WIDTH, GDN_WIDTH, GDN_WIDTH, GDN_HEADS, GDN_HEADS,
             D_MODEL, D_MODEL)
    idx = np.cumsum(sizes)[:-1].tolist()
    return jnp.split(proj, idx, axis=-1)


def _fox_logits(q, k, Fq, Fk):
    s = jnp.einsum('bqhd,bkhd->bhqk', q, k, preferred_element_type=F32) * (FOX_HEAD_DIM ** -0.5)
    return s + (jnp.swapaxes(Fq, 1, 2)[..., :, None] - jnp.swapaxes(Fk, 1, 2)[..., None, :])


def _fox_prompt_attention(q, k, v, F):
    B, T, H, Dh = q.shape
    nb = T // FOX_BLOCK_Q
    qb = jnp.swapaxes(q.reshape(B, nb, FOX_BLOCK_Q, H, Dh), 0, 1)
    Fb = jnp.swapaxes(F.reshape(B, nb, FOX_BLOCK_Q, H), 0, 1)
    k_pos = jnp.arange(T)

    def block(args):
        i, qi, Fi = args
        s = _fox_logits(qi, k, Fi, F)
        q_pos = i * FOX_BLOCK_Q + jnp.arange(FOX_BLOCK_Q)
        s = jnp.where(k_pos[None, None, None, :] <= q_pos[None, None, :, None], s, -jnp.inf)
        p = jax.nn.softmax(s, axis=-1).astype(v.dtype)
        return jnp.einsum('bhqk,bkhd->bqhd', p, v)

    o = lax.map(block, (jnp.arange(nb), qb, Fb))
    return jnp.swapaxes(o, 0, 1).reshape(B, T, H, Dh)


def _fox_sample_attention(q, k, v, F, k_past, v_past, logf_past):
    T = q.shape[1]
    P = k_past.shape[1]
    lp = logf_past.astype(F32)
    F_past = lp - lax.cumsum(lp, axis=1, reverse=True)
    s_past = _fox_logits(q, k_past.astype(q.dtype), F, F_past)
    s_new = _fox_logits(q, k, F, F)
    causal = jnp.tril(jnp.ones((T, T), bool))
    s_new = jnp.where(causal, s_new, -jnp.inf)
    p = jax.nn.softmax(jnp.concatenate([s_past, s_new], axis=-1), axis=-1).astype(v.dtype)
    return (jnp.einsum('bhqk,bkhd->bqhd', p[..., :P], v_past.astype(v.dtype))
            + jnp.einsum('bhqk,bkhd->bqhd', p[..., P:], v))


def _fox_branch(q, k, v, f_logit, b_forget, past):
    B, T = q.shape[:2]
    shp = (B, T, FOX_HEADS, FOX_HEAD_DIM)
    q, k, v = q.reshape(shp), k.reshape(shp), v.reshape(shp)
    logf = jax.nn.log_sigmoid(f_logit.astype(F32) + b_forget.astype(F32))
    F = jnp.cumsum(logf, axis=1)
    if past is None:
        o = _fox_prompt_attention(q, k, v, F)
    else:
        o = _fox_sample_attention(q, k, v, F, past[0], past[1], past[2])
    return o.reshape(B, T, FOX_WIDTH), (k, v, logf)


def _causal_conv(u, buf, w):
    T = u.shape[1]
    full = jnp.concatenate([buf.astype(u.dtype), u], axis=1)
    y = full[:, 0:T] * w[0]
    for i in range(1, GDN_CONV):
        y = y + full[:, i:i + T] * w[i]
    return jax.nn.silu(y), full[:, T:]


def _gated_delta_rule(q, k, v, g, beta, s0):
    B, T, H, DK = q.shape
    C = GDN_CHUNK
    pad = (-T) % C
    NC = (T + pad) // C

    def to_chunks(a):
        a = jnp.pad(a, [(0, 0), (0, pad)] + [(0, 0)] * (a.ndim - 2))
        a = a.reshape((B, NC, C) + a.shape[2:])
        return jnp.moveaxis(a, 3, 1)

    q = to_chunks(q) * (DK ** -0.5)
    k = to_chunks(k)
    v = to_chunks(v)
    g = to_chunks(g)
    beta = to_chunks(beta)
    gc = jnp.cumsum(g, axis=-1)
    incl = jnp.tril(jnp.ones((C, C), bool))
    strict = jnp.tril(jnp.ones((C, C), bool), -1)
    decay = jnp.where(incl, jnp.exp(jnp.where(incl, gc[..., :, None] - gc[..., None, :], 0.0)), 0.0)
    kb = k * beta[..., None]
    L = jnp.where(strict, jnp.einsum('bhnid,bhnjd->bhnij', kb, k) * decay, 0.0)
    eye = jnp.eye(C, dtype=F32)
    tinv = lax.linalg.triangular_solve(eye + L, jnp.broadcast_to(eye, L.shape),
                                       left_side=True, lower=True, unit_diagonal=True)
    u = jnp.einsum('bhnij,bhnjd->bhnid', tinv, v * beta[..., None])
    w = jnp.einsum('bhnij,bhnjd->bhnid', tinv, kb * jnp.exp(gc)[..., None])
    a_intra = jnp.where(incl, jnp.einsum('bhnid,bhnjd->bhnij', q, k) * decay, 0.0)
    q_dec = q * jnp.exp(gc)[..., None]
    k_dec = k * jnp.exp(gc[..., -1:] - gc)[..., None]
    g_tot = jnp.exp(gc[..., -1])
    xs = tuple(jnp.moveaxis(t, 2, 0) for t in (u, w, a_intra, q_dec, k_dec, g_tot))

    def step(S, inp):
        u_i, w_i, a_i, qd_i, kd_i, gt_i = inp
        v_new = u_i - jnp.einsum('bhcd,bhde->bhce', w_i, S)
        o_i = jnp.einsum('bhcd,bhde->bhce', qd_i, S) + jnp.einsum('bhcs,bhse->bhce', a_i, v_new)
        S = S * gt_i[..., None, None] + jnp.einsum('bhcd,bhce->bhde', kd_i, v_new)
        return S, o_i

    S, o = lax.scan(step, s0, xs)
    o = jnp.moveaxis(o, 0, 2).reshape(B, H, NC * C, -1)[:, :, :T]
    return jnp.swapaxes(o, 1, 2), S


def _gdn_branch(q, k, v, z, a, b, conv_buf, s0, conv_w, a_log, dt_bias, norm_w):
    B, T = q.shape[:2]
    dtype = z.dtype
    qkv, new_buf = _causal_conv(jnp.concatenate([q, k, v], axis=-1), conv_buf, conv_w)
    qc, kc, vc = jnp.split(qkv, 3, axis=-1)
    hs = (B, T, GDN_HEADS, GDN_HEAD_DIM)
    qn = _l2norm(qc.reshape(hs))
    kn = _l2norm(kc.reshape(hs))
    vf = vc.reshape(hs).astype(F32)
    g = -jnp.exp(a_log.astype(F32)) * jax.nn.softplus(a.astype(F32) + dt_bias.astype(F32))
    beta = jax.nn.sigmoid(b.astype(F32))
    o, s_new = _gated_delta_rule(qn, kn, vf, g, beta, s0.astype(F32))
    o = o * lax.rsqrt(jnp.mean(o * o, axis=-1, keepdims=True) + NORM_EPS) * norm_w.astype(F32)
    o = o * jax.nn.silu(z.reshape(hs).astype(F32))
    return o.reshape(B, T, GDN_WIDTH).astype(dtype), new_buf, s_new


def _experts_dense(hf, combine, w_g, w_u, w_d):
    N, D = hf.shape
    blk = min(N, MOE_TOKEN_BLOCK)
    nb = -(-N // blk)
    pad = nb * blk - N
    hp = jnp.pad(hf, ((0, pad), (0, 0))).reshape(nb, blk, D)
    cp = jnp.pad(combine, ((0, pad), (0, 0))).reshape(nb, blk, N_EXPERTS)

    def run(args):
        hb, cb = args
        act = (jax.nn.silu(jnp.einsum('nd,edf->nef', hb, w_g))
               * jnp.einsum('nd,edf->nef', hb, w_u) * cb[..., None])
        return jnp.einsum('nef,efd->nd', act, w_d)

    return lax.map(run, (hp, cp)).reshape(nb * blk, D)[:N]


def _hier_moe(h, w_grp, b_grp, w_rt, b_rt, w_g, w_u, w_d):
    B, T, D = h.shape
    hf = h.reshape(B * T, D)
    grp_logits = jnp.dot(hf, w_grp, preferred_element_type=F32) + b_grp.astype(F32)
    grp_prob = jax.nn.softmax(grp_logits, axis=-1)
    grp_p, grp_i = lax.top_k(grp_prob, 1)
    exp_logits = (jnp.dot(hf, w_rt, preferred_element_type=F32) + b_rt.astype(F32)
                  ).reshape(-1, N_GROUPS, EXPERTS_PER_GROUP)
    in_grp = jnp.einsum('ng,nge->ne', jax.nn.one_hot(grp_i[:, 0], N_GROUPS, dtype=F32), exp_logits)
    top_l, top_i = lax.top_k(in_grp, TOP_K_IN_GROUP)
    w_sel = grp_p * jax.nn.softmax(top_l, axis=-1)
    e_idx = grp_i * EXPERTS_PER_GROUP + top_i
    combine = jnp.einsum('nk,nke->ne', w_sel, jax.nn.one_hot(e_idx, N_EXPERTS, dtype=F32))
    y = _experts_dense(hf, combine.astype(h.dtype), w_g, w_u, w_d)
    return y.reshape(B, T, D)


def _layer(x, c, p, fox_past, conv_buf, gdn_s0):
    shift, scale, gate = _adaln(c, p['w_ada_mix'], p['b_ada_mix'])
    h = x * (1 + scale) + shift
    proj = jnp.einsum('btd,de->bte', h, p['w_in'])
    fq, fk, fv, ff, gq, gk, gv, gz, ga, gb, gate_f, gate_g = _split_in(proj)
    o_fox, fox_rows = _fox_branch(fq, fk, fv, ff, p['b_forget'], fox_past)
    o_gdn, new_conv, new_s = _gdn_branch(gq, gk, gv, gz, ga, gb, conv_buf, gdn_s0, p['gdn_conv_w'],
                                         p['gdn_a_log'], p['gdn_dt_bias'], p['gdn_norm_w'])
    merged = (jax.nn.sigmoid(gate_f) * jnp.einsum('btc,cd->btd', o_fox, p['w_up_fox'])
              + jax.nn.sigmoid(gate_g) * jnp.einsum('btc,cd->btd', o_gdn, p['w_up_gdn']))
    mix = jnp.einsum('btd,de->bte', merged, p['w_out'])
    x = _layer_norm(DEEPNORM_ALPHA * x + (1 + gate) * mix, p['ln1_g'], p['ln1_b'])
    shift, scale, gate = _adaln(c, p['w_ada_ffn'], p['b_ada_ffn'])
    h = x * (1 + scale) + shift
    ffn = _hier_moe(h, p['w_group_router'], p['b_group_router'], p['w_expert_router'],
                    p['b_expert_router'], p['w_expert_gate'], p['w_expert_up'], p['w_expert_down'])
    x = _layer_norm(DEEPNORM_ALPHA * x + (1 + gate) * ffn, p['ln2_g'], p['ln2_b'])
    return x, fox_rows, new_s, new_conv


def setup_inputs(seed: int = 0) -> dict:
    key = jax.random.key(seed)
    ks = jax.random.split(key, 64)
    counter = [0]

    def nxt():
        counter[0] += 1
        return ks[counter[0] - 1]

    def nrm(shape, scale):
        return jax.random.normal(nxt(), shape, F32) * scale

    L, D = DEPTH, D_MODEL
    sd = D ** -0.5
    n_pages = PAST_LEN // PAGE_SIZE
    n_phys = (DEC_BATCH * n_pages * 5) // 4
    x_prompt = nrm((BATCH, SEQ, D), 1.0)
    x_sample = nrm((DEC_BATCH, DEC_SEQ, D), 1.0)
    b_forget = jax.random.uniform(nxt(), (L, FOX_HEADS), F32, 1.0, 6.0)
    cache_fox_k = nrm((L, n_phys, PAGE_SIZE, FOX_HEADS, FOX_HEAD_DIM), 1.0)
    cache_fox_v = nrm((L, n_phys, PAGE_SIZE, FOX_HEADS, FOX_HEAD_DIM), DEEPNORM_BETA)
    cache_fox_logf = jax.nn.log_sigmoid(b_forget[:, None, None, :]
                                        + nrm((L, n_phys, PAGE_SIZE, FOX_HEADS), 1.0))
    state_gdn = nrm((L, DEC_BATCH, GDN_HEADS, GDN_HEAD_DIM, GDN_HEAD_DIM), 0.05)
    state_gdn_conv = nrm((L, DEC_BATCH, GDN_CONV - 1, 3 * GDN_WIDTH), 1.0)
    perm = jax.random.permutation(nxt(), n_phys)
    page_table = perm[:DEC_BATCH * n_pages].reshape(DEC_BATCH, n_pages).astype(jnp.int32)
    c_prompt = nrm((BATCH, D), 1.0)
    c_sample = nrm((DEC_BATCH, D), 1.0)
    w_ada_mix = nrm((L, D, 3 * D), sd * 0.25)
    b_ada_mix = nrm((L, 3 * D), 0.02)
    w_in = jnp.concatenate([
        nrm((L, D, FOX_WIDTH), sd),
        nrm((L, D, FOX_WIDTH), sd),
        nrm((L, D, FOX_WIDTH), sd * DEEPNORM_BETA),
        nrm((L, D, FOX_HEADS), sd),
        nrm((L, D, GDN_WIDTH), sd),
        nrm((L, D, GDN_WIDTH), sd),
        nrm((L, D, GDN_WIDTH), sd * DEEPNORM_BETA),
        nrm((L, D, GDN_WIDTH), sd),
        nrm((L, D, GDN_HEADS), sd),
        nrm((L, D, GDN_HEADS), sd),
        nrm((L, D, 2 * D), sd),
    ], axis=-1)
    gdn_conv_w = nrm((L, GDN_CONV, 3 * GDN_WIDTH), GDN_CONV ** -0.5)
    gdn_a_log = jnp.log(jax.random.uniform(nxt(), (L, GDN_HEADS), F32, 1.0, 16.0))
    dt = jnp.exp(jax.random.uniform(nxt(), (L, GDN_HEADS), F32, np.log(1e-3), np.log(1e-1)))
    gdn_dt_bias = dt + jnp.log(-jnp.expm1(-dt))
    gdn_norm_w = 1.0 + nrm((L, GDN_HEAD_DIM), 0.02)
    w_up_fox = nrm((L, FOX_WIDTH, D), FOX_WIDTH ** -0.5)
    w_up_gdn = nrm((L, GDN_WIDTH, D), GDN_WIDTH ** -0.5)
    w_out = nrm((L, D, D), sd * DEEPNORM_BETA)
    ln1_g = 1.0 + nrm((L, D), 0.02)
    ln1_b = nrm((L, D), 0.02)
    w_ada_ffn = nrm((L, D, 3 * D), sd * 0.25)
    b_ada_ffn = nrm((L, 3 * D), 0.02)
    w_group_router = nrm((L, D, N_GROUPS), sd)
    b_group_router = nrm((L, N_GROUPS), 0.01)
    w_expert_router = nrm((L, D, N_EXPERTS), sd)
    b_expert_router = nrm((L, N_EXPERTS), 0.01)
    w_expert_gate = nrm((L, N_EXPERTS, D, EXPERT_FF), sd)
    w_expert_up = nrm((L, N_EXPERTS, D, EXPERT_FF), sd)
    w_expert_down = nrm((L, N_EXPERTS, EXPERT_FF, D), EXPERT_FF ** -0.5 * DEEPNORM_BETA)
    ln2_g = 1.0 + nrm((L, D), 0.02)
    ln2_b = nrm((L, D), 0.02)
    return {
        'x_prompt': x_prompt, 'x_sample': x_sample,
        'cache_fox_k': cache_fox_k, 'cache_fox_v': cache_fox_v, 'cache_fox_logf': cache_fox_logf,
        'state_gdn': state_gdn, 'state_gdn_conv': state_gdn_conv, 'page_table': page_table,
        'c_prompt': c_prompt, 'c_sample': c_sample,
        'w_ada_mix': w_ada_mix, 'b_ada_mix': b_ada_mix, 'w_in': w_in, 'b_forget': b_forget,
        'gdn_conv_w': gdn_conv_w, 'gdn_a_log': gdn_a_log, 'gdn_dt_bias': gdn_dt_bias,
        'gdn_norm_w': gdn_norm_w, 'w_up_fox': w_up_fox, 'w_up_gdn': w_up_gdn, 'w_out': w_out,
        'ln1_g': ln1_g, 'ln1_b': ln1_b, 'w_ada_ffn': w_ada_ffn, 'b_ada_ffn': b_ada_ffn,
        'w_group_router': w_group_router, 'b_group_router': b_group_router,
        'w_expert_router': w_expert_router, 'b_expert_router': b_expert_router,
        'w_expert_gate': w_expert_gate, 'w_expert_up': w_expert_up, 'w_expert_down': w_expert_down,
        'ln2_g': ln2_g, 'ln2_b': ln2_b,
    }


def reference(x_prompt, x_sample, cache_fox_k, cache_fox_v, cache_fox_logf, state_gdn, state_gdn_conv,
              page_table, c_prompt, c_sample, w_ada_mix, b_ada_mix, w_in, b_forget, gdn_conv_w,
              gdn_a_log, gdn_dt_bias, gdn_norm_w, w_up_fox, w_up_gdn, w_out, ln1_g, ln1_b,
              w_ada_ffn, b_ada_ffn, w_group_router, b_group_router, w_expert_router, b_expert_router,
              w_expert_gate, w_expert_up, w_expert_down, ln2_g, ln2_b):
    y_prompt, y_sample = x_prompt, x_sample
    n_seq = page_table.shape[0]
    kp_l, vp_l, lfp_l, sp_l, cvp_l = [], [], [], [], []
    ks_l, vs_l, lfs_l, ss_l, cvs_l = [], [], [], [], []
    for l in range(DEPTH):
        p = dict(w_ada_mix=w_ada_mix[l], b_ada_mix=b_ada_mix[l], w_in=w_in[l], b_forget=b_forget[l],
                 gdn_conv_w=gdn_conv_w[l], gdn_a_log=gdn_a_log[l], gdn_dt_bias=gdn_dt_bias[l],
                 gdn_norm_w=gdn_norm_w[l], w_up_fox=w_up_fox[l], w_up_gdn=w_up_gdn[l], w_out=w_out[l],
                 ln1_g=ln1_g[l], ln1_b=ln1_b[l], w_ada_ffn=w_ada_ffn[l], b_ada_ffn=b_ada_ffn[l],
                 w_group_router=w_group_router[l], b_group_router=b_group_router[l],
                 w_expert_router=w_expert_router[l], b_expert_router=b_expert_router[l],
                 w_expert_gate=w_expert_gate[l], w_expert_up=w_expert_up[l],
                 w_expert_down=w_expert_down[l], ln2_g=ln2_g[l], ln2_b=ln2_b[l])
        bp = y_prompt.shape[0]
        conv0 = jnp.zeros((bp, GDN_CONV - 1, 3 * GDN_WIDTH), y_prompt.dtype)
        s0 = jnp.zeros((bp, GDN_HEADS, GDN_HEAD_DIM, GDN_HEAD_DIM), F32)
        y_prompt, rows_p, s_p, cv_p = _layer(y_prompt, c_prompt, p, None, conv0, s0)
        k_past = cache_fox_k[l, page_table].reshape(n_seq, -1, FOX_HEADS, FOX_HEAD_DIM)
        v_past = cache_fox_v[l, page_table].reshape(n_seq, -1, FOX_HEADS, FOX_HEAD_DIM)
        lf_past = cache_fox_logf[l, page_table].reshape(n_seq, -1, FOX_HEADS)
        y_sample, rows_s, s_s, cv_s = _layer(y_sample, c_sample, p, (k_past, v_past, lf_past),
                                             state_gdn_conv[l], state_gdn[l])
        kp_l.append(rows_p[0]); vp_l.append(rows_p[1]); lfp_l.append(rows_p[2])
        sp_l.append(s_p); cvp_l.append(cv_p)
        ks_l.append(rows_s[0]); vs_l.append(rows_s[1]); lfs_l.append(rows_s[2])
        ss_l.append(s_s); cvs_l.append(cv_s)
    new_fox_k_prompt = jnp.stack(kp_l)
    new_fox_v_prompt = jnp.stack(vp_l)
    new_fox_logf_prompt = jnp.stack(lfp_l)
    new_gdn_state_prompt = jnp.stack(sp_l)
    new_gdn_conv_prompt = jnp.stack(cvp_l)
    new_fox_k_sample = jnp.stack(ks_l)
    new_fox_v_sample = jnp.stack(vs_l)
    new_fox_logf_sample = jnp.stack(lfs_l)
    new_gdn_state_sample = jnp.stack(ss_l)
    new_gdn_conv_sample = jnp.stack(cvs_l)
    return (y_prompt, y_sample, new_fox_k_prompt, new_fox_v_prompt, new_fox_logf_prompt,
            new_gdn_state_prompt, new_gdn_conv_prompt, new_fox_k_sample, new_fox_v_sample,
            new_fox_logf_sample, new_gdn_state_sample, new_gdn_conv_sample)
```

```python
import functools

import jax
import jax.numpy as jnp
from jax import lax
from jax.experimental import pallas as pl
from jax.experimental.pallas import tpu as pltpu

F32 = jnp.float32
BF16 = jnp.bfloat16

D_MODEL = 1024
FOX_HEADS = 8
FOX_HEAD_DIM = 64
FOX_WIDTH = FOX_HEADS * FOX_HEAD_DIM
GDN_HEADS = 4
GDN_HEAD_DIM = 128
GDN_WIDTH = GDN_HEADS * GDN_HEAD_DIM
GDN_CONV = 4
N_GROUPS = 4
EXPERTS_PER_GROUP = 8
N_EXPERTS = N_GROUPS * EXPERTS_PER_GROUP
EXPERT_FF = 512
PAGE_SIZE = 128
DEEPNORM_ALPHA = 2.0 ** 0.25
LN_EPS = 1e-5
NORM_EPS = 1e-6
LOG2E = 1.4426950408889634
LANES = 128
VMEM_LIMIT = 56 * 1024 * 1024

_C_FQ, _C_FK, _C_FV, _C_GQKV, _C_GZ, _C_GF, _C_GG, _C_END = 0, 512, 1024, 1536, 3072, 3584, 4608, 5632


def _params(sem):
    return pltpu.CompilerParams(dimension_semantics=sem, vmem_limit_bytes=VMEM_LIMIT)


def _dot(a, b):
    return jnp.dot(a, b, preferred_element_type=F32)


def _dot_nt(a, b):
    return lax.dot_general(a, b, (((1,), (1,)), ((), ())), preferred_element_type=F32)


def _dot_tn(a, b):
    return lax.dot_general(a, b, (((0,), (0,)), ((), ())), preferred_element_type=F32)


def _split2(x):
    hi = x.astype(BF16)
    lo = (x - hi.astype(F32)).astype(BF16)
    return hi, lo


def _split3(x):
    a = x.astype(BF16)
    r = x - a.astype(F32)
    b = r.astype(BF16)
    c = (r - b.astype(F32)).astype(BF16)
    return a, b, c


def _dot3(a, b):
    ah, al = _split2(a)
    bh, bl = _split2(b)
    return _dot(ah, bh) + (_dot(ah, bl) + _dot(al, bh))


def _dot3_nt(a, b):
    ah, al = _split2(a)
    bh, bl = _split2(b)
    return _dot_nt(ah, bh) + (_dot_nt(ah, bl) + _dot_nt(al, bh))


def _silu(x):
    return x * jax.nn.sigmoid(x)


def _softplus(x):
    return jnp.maximum(x, 0.0) + jnp.log1p(jnp.exp(-jnp.abs(x)))


def _log_sigmoid(x):
    return jnp.minimum(x, 0.0) - jnp.log1p(jnp.exp(-jnp.abs(x)))


def _ada_kernel(c_ref, w_ref, b_ref, o_ref):
    s = _silu(c_ref[...])
    o_ref[...] = jnp.dot(s, w_ref[...], preferred_element_type=F32,
                         precision=lax.Precision.HIGHEST) + b_ref[...]


def _adaln(c, w, b):
    rows, d = c.shape
    n = w.shape[1]
    bn = 512
    return pl.pallas_call(
        _ada_kernel,
        grid=(n // bn,),
        in_specs=[pl.BlockSpec((rows, d), lambda j: (0, 0)),
                  pl.BlockSpec((d, bn), lambda j: (0, j)),
                  pl.BlockSpec((1, bn), lambda j: (0, j))],
        out_specs=pl.BlockSpec((rows, bn), lambda j: (0, j)),
        out_shape=jax.ShapeDtypeStruct((rows, n), F32),
        compiler_params=_params(("arbitrary",)),
        name="adaln",
    )(c, w, b.reshape(1, n))


N_SMALL = 16


def _pack_w_in(w_in_t):
    wide = jnp.concatenate([w_in_t[0:1536], w_in_t[1544:3592], w_in_t[3600:5648]], axis=0).astype(BF16)
    small = jnp.concatenate([w_in_t[1536:1544], w_in_t[3592:3600],
                             jnp.zeros((LANES - N_SMALL, w_in_t.shape[1]), F32)], axis=0)
    sh = small.astype(BF16)
    sl = (small - sh.astype(F32)).astype(BF16)
    return wide, sh, sl


def _param_rows(b_forget, a_log, dt_bias):
    z = jnp.zeros((LANES,), F32)
    bias = z.at[0:8].set(b_forget).at[8:12].set(dt_bias)
    nega = z.at[8:12].set(-jnp.exp(a_log))
    rows = jnp.zeros((8, LANES), F32).at[0].set(bias).at[1].set(nega)
    cols = jnp.zeros((N_SMALL, LANES), F32).at[:, 0].set(bias[0:N_SMALL]).at[:, 1].set(nega[0:N_SMALL])
    return rows, cols


def _small_act(z, idx, nega):
    return jnp.where(idx < 8, _log_sigmoid(z), jnp.where(idx < 12, nega * _softplus(z), jax.nn.sigmoid(z)))


def _proj_kernel(x_ref, shift_ref, scale_ref, w_ref, wsh_ref, wsl_ref, prow_ref, pcol_ref, *refs,
                 seg_len, kv_transposed):
    if kv_transposed:
        (q_ref, vb_ref, gqkv_ref, gz_ref, gf_ref, gg_ref, small_ref, smallt_ref, ft_ref,
         ktf_ref, ktb_ref, vtf_ref, carry_ref) = refs
    else:
        (q_ref, vb_ref, gqkv_ref, gz_ref, gf_ref, gg_ref, small_ref, smallt_ref, ft_ref,
         kf_ref, kb_ref, vf_ref, carry_ref) = refs
    i = pl.program_id(0)
    tm = x_ref.shape[0]

    @pl.when(i == 0)
    def _():
        carry_ref[...] = jnp.zeros_like(carry_ref)

    h = x_ref[...] * (1.0 + scale_ref[...]) + shift_ref[...]
    hb, hl = _split2(h)
    seg = lambda lo, hi: w_ref[lo:hi, :]

    q_ref[...] = (_dot_nt(hb, seg(_C_FQ, _C_FK)) * (LOG2E * FOX_HEAD_DIM ** -0.5)).astype(BF16)
    v = _dot_nt(hb, seg(_C_FV, _C_GQKV))
    vb_ref[...] = v.astype(BF16)
    if kv_transposed:
        kt = _dot_nt(seg(_C_FK, _C_FV), hb)
        ktf_ref[...] = kt
        ktb_ref[...] = kt.astype(BF16)
        vtf_ref[...] = _dot_nt(seg(_C_FV, _C_GQKV), hb)
    else:
        k = _dot_nt(hb, seg(_C_FK, _C_FV))
        kf_ref[...] = k
        kb_ref[...] = k.astype(BF16)
        vf_ref[...] = v
    gqkv_ref[...] = _dot_nt(hb, seg(_C_GQKV, _C_GZ))
    gz_ref[...] = _dot_nt(hb, seg(_C_GZ, _C_GF))
    gf_ref[...] = jax.nn.sigmoid(_dot_nt(hb, seg(_C_GF, _C_GG))).astype(BF16)
    gg_ref[...] = jax.nn.sigmoid(_dot_nt(hb, seg(_C_GG, _C_END))).astype(BF16)

    wsh = wsh_ref[...]
    z = _dot_nt(hb, wsh) + (_dot_nt(hb, wsl_ref[...]) + _dot_nt(hl, wsh)) + prow_ref[0:1, :]
    lane = lax.broadcasted_iota(jnp.int32, z.shape, 1)
    small_ref[...] = _small_act(z, lane, prow_ref[1:2, :])

    ws16h = wsh_ref[0:N_SMALL, :]
    zt = (_dot_nt(ws16h, hb) + (_dot_nt(wsl_ref[0:N_SMALL, :], hb) + _dot_nt(ws16h, hl))) + pcol_ref[:, 0:1]
    rid = lax.broadcasted_iota(jnp.int32, zt.shape, 0)
    small_t = _small_act(zt, rid, pcol_ref[:, 1:2])
    smallt_ref[...] = small_t
    row = lax.broadcasted_iota(jnp.int32, (tm, tm), 0)
    col = lax.broadcasted_iota(jnp.int32, (tm, tm), 1)
    keep = row <= col
    if seg_len is not None:
        keep = keep & ((row // seg_len) == (col // seg_len))
    tri = jnp.where(keep, 1.0, 0.0).astype(BF16)
    l0, l1, l2 = _split3(jnp.where(rid < FOX_HEADS, small_t, 0.0))
    fcum = _dot(l0, tri) + (_dot(l1, tri) + _dot(l2, tri))
    if seg_len is None:
        fcum = fcum + carry_ref[:, 0:1]
        carry_ref[...] = jnp.broadcast_to(fcum[:, tm - 1:tm], carry_ref.shape)
    ft_ref[...] = fcum * LOG2E


def _project(x, shift, scale, wide, wsh, wsl, prow, pcol, *, tm, seg_len, kv_transposed):
    t, d = x.shape
    nt = t // tm
    per_tok = shift.shape[0] != 1
    mod_spec = (pl.BlockSpec((tm, d), lambda i: (i, 0)) if per_tok
                else pl.BlockSpec((1, d), lambda i: (0, 0)))
    const = lambda a: pl.BlockSpec(a.shape, lambda i: (0, 0))
    tile = lambda n: pl.BlockSpec((tm, n), lambda i: (i, 0))
    ttile = lambda n: pl.BlockSpec((n, tm), lambda i: (0, i))
    tok = lambda n, dt: (tile(n), jax.ShapeDtypeStruct((t, n), dt))
    tra = lambda n, dt: (ttile(n), jax.ShapeDtypeStruct((n, t), dt))
    outs = {'q': tok(FOX_WIDTH, BF16), 'vb': tok(FOX_WIDTH, BF16), 'gqkv': tok(3 * GDN_WIDTH, F32),
            'gz': tok(GDN_WIDTH, F32), 'gf': tok(D_MODEL, BF16), 'gg': tok(D_MODEL, BF16),
            'small': tok(LANES, F32), 'small_t': tra(N_SMALL, F32), 'ft': tra(N_SMALL, F32)}
    if kv_transposed:
        outs['ktf'] = tra(FOX_WIDTH, F32)
        outs['ktb'] = (pl.BlockSpec((None, FOX_WIDTH, tm), lambda i: (i, 0, 0)),
                       jax.ShapeDtypeStruct((nt, FOX_WIDTH, tm), BF16))
        outs['vtf'] = tra(FOX_WIDTH, F32)
    else:
        outs['kf'] = tok(FOX_WIDTH, F32)
        outs['kb'] = tok(FOX_WIDTH, BF16)
        outs['vf'] = tok(FOX_WIDTH, F32)
    res = pl.pallas_call(
        functools.partial(_proj_kernel, seg_len=seg_len, kv_transposed=kv_transposed),
        grid=(nt,),
        in_specs=[tile(d), mod_spec, mod_spec, const(wide), const(wsh), const(wsl), const(prow), const(pcol)],
        out_specs=[v[0] for v in outs.values()],
        out_shape=[v[1] for v in outs.values()],
        scratch_shapes=[pltpu.VMEM((N_SMALL, LANES), F32)],
        compiler_params=_params(("arbitrary",)),
        name="in_proj",
    )(x, shift, scale, wide, wsh, wsl, prow, pcol)
    return dict(zip(outs.keys(), res))


NEG_BIG = -1e30


def _fox_prompt_kernel(q_ref, kt_ref, v_ref, fk_ref, o_ref, m_ref, l_ref, acc_ref, *, tk):
    qi = pl.program_id(1)
    tq = q_ref.shape[0]
    q = q_ref[...]
    lane = lax.broadcasted_iota(jnp.int32, q.shape, 1)
    n_diag = tq // tk
    n_full = qi * n_diag
    row = lax.broadcasted_iota(jnp.int32, (tq, tk), 0)
    col = lax.broadcasted_iota(jnp.int32, (tq, tk), 1)

    def step(h, qh, j, masked_from):
        off = pl.multiple_of(j * tk, tk)
        v = v_ref[pl.ds(off, tk), :]
        s = _dot(qh, kt_ref[j]) - fk_ref[h, pl.ds(j, 1), :]
        if masked_from is not None:
            s = jnp.where(col + masked_from * tk <= row, s, NEG_BIG)
        m_prev = m_ref[h]
        m_new = jnp.maximum(m_prev, jnp.max(s, axis=-1, keepdims=True))
        alpha = jnp.exp2(m_prev - m_new)
        p = jnp.exp2(s - m_new)
        l_ref[h] = alpha * l_ref[h] + jnp.sum(p, axis=-1, keepdims=True)
        acc_ref[h] = alpha * acc_ref[h] + _dot(p.astype(BF16), v)
        m_ref[h] = m_new

    for h in range(2):
        qh = jnp.where((lane < FOX_HEAD_DIM) == (h == 0), q, jnp.zeros_like(q))
        m_ref[h] = jnp.full(m_ref.shape[1:], NEG_BIG, F32)
        l_ref[h] = jnp.zeros(l_ref.shape[1:], F32)
        acc_ref[h] = jnp.zeros(acc_ref.shape[1:], F32)

        def body(j, c, h=h, qh=qh):
            step(h, qh, j, None)
            return c

        lax.fori_loop(0, n_full, body, 0)
        for jj in range(n_diag):
            step(h, qh, n_full + jj, jj)

    o0 = acc_ref[0] / l_ref[0]
    o1 = acc_ref[1] / l_ref[1]
    o_ref[...] = jnp.where(lane < FOX_HEAD_DIM, o0, o1).astype(o_ref.dtype)


def _fox_prompt(q, kt, v, f_rows, *, tq):
    t = q.shape[0]
    nk, _, tk = kt.shape
    n_pairs = FOX_WIDTH // LANES
    fk = f_rows.reshape(n_pairs, 2, nk, tk)
    return pl.pallas_call(
        functools.partial(_fox_prompt_kernel, tk=tk),
        grid=(n_pairs, t // tq),
        in_specs=[pl.BlockSpec((tq, LANES), lambda p, i: (i, p)),
                  pl.BlockSpec((nk, LANES, tk), lambda p, i: (0, p, 0)),
                  pl.BlockSpec((t, LANES), lambda p, i: (0, p)),
                  pl.BlockSpec((None, 2, t // tk, tk), lambda p, i: (p, 0, 0, 0))],
        out_specs=pl.BlockSpec((tq, LANES), lambda p, i: (i, p)),
        out_shape=jax.ShapeDtypeStruct((t, FOX_WIDTH), BF16),
        scratch_shapes=[pltpu.VMEM((2, tq, 1), F32), pltpu.VMEM((2, tq, 1), F32),
                        pltpu.VMEM((2, tq, LANES), F32)],
        compiler_params=_params(("arbitrary", "arbitrary")),
        name="fox_prompt",
    )(q, kt, v, fk)


PAGES_PER_STEP = 8


def _suffix_kernel(lf_ref, m_ref, o_ref):
    l0, l1, l2 = _split3(lf_ref[...] * LOG2E)
    m = m_ref[...]
    o_ref[...] = _dot(l0, m) + (_dot(l1, m) + _dot(l2, m))


def _page_suffix(logf_t):
    n_phys = logf_t.shape[0]
    rows = n_phys * FOX_HEADS
    src = jnp.arange(PAGE_SIZE)
    mat = (src[:, None] >= src[None, :]).astype(BF16)
    br = 4096
    out = pl.pallas_call(
        _suffix_kernel,
        grid=(rows // br,),
        in_specs=[pl.BlockSpec((br, PAGE_SIZE), lambda i: (i, 0)),
                  pl.BlockSpec((PAGE_SIZE, PAGE_SIZE), lambda i: (0, 0))],
        out_specs=pl.BlockSpec((br, PAGE_SIZE), lambda i: (i, 0)),
        out_shape=jax.ShapeDtypeStruct((rows, PAGE_SIZE), F32),
        compiler_params=_params(("arbitrary",)),
        name="page_suffix",
    )(logf_t.reshape(rows, PAGE_SIZE), mat)
    return out.reshape(n_phys, FOX_HEADS, PAGE_SIZE)


def _fox_sample_kernel(pt_ref, q_ref, kn_ref, vn_ref, fn_ref, *refs):
    g_pages = PAGES_PER_STEP
    k_refs = refs[0:g_pages]
    v_refs = refs[g_pages:2 * g_pages]
    i_refs = refs[2 * g_pages:3 * g_pages]
    o_ref, qbd_ref, m_ref, l_ref, acc_ref, base_ref = refs[3 * g_pages:]
    del pt_ref
    j = pl.program_id(1)
    nh, nq, hd = FOX_HEADS, q_ref.shape[0], FOX_HEAD_DIM
    rows = nh * nq
    lane_w = lax.broadcasted_iota(jnp.int32, (rows, FOX_WIDTH), 1)
    row_w = lax.broadcasted_iota(jnp.int32, (rows, FOX_WIDTH), 0)
    own = (lane_w // hd) == (row_w // nq)

    @pl.when(j == 0)
    def _():
        q = q_ref[...]
        qrep = jnp.broadcast_to(q[None], (nh, nq, FOX_WIDTH)).reshape(rows, FOX_WIDTH)
        qbd_ref[...] = jnp.where(own, qrep, jnp.zeros_like(qrep))
        m_ref[...] = jnp.full(m_ref.shape, NEG_BIG, F32)
        l_ref[...] = jnp.zeros(l_ref.shape, F32)
        acc_ref[...] = jnp.zeros(acc_ref.shape, F32)
        base_ref[...] = jnp.zeros(base_ref.shape, F32)

    qbd = qbd_ref[...]
    lane_p = lax.broadcasted_iota(jnp.int32, (nh, PAGE_SIZE), 1)

    def update(s, v, v_transposed):
        m_prev = m_ref[...]
        m_new = jnp.maximum(m_prev, jnp.max(s, axis=-1, keepdims=True))
        alpha = jnp.exp2(m_prev - m_new)
        p = jnp.exp2(s - m_new)
        l_ref[...] = alpha * l_ref[...] + jnp.sum(p, axis=-1, keepdims=True)
        pb = p.astype(BF16)
        pv = _dot_nt(pb, v) if v_transposed else _dot(pb, v)
        acc_ref[...] = alpha * acc_ref[...] + pv
        m_ref[...] = m_new

    base = base_ref[...]
    scores, values = [], []
    for g in range(g_pages):
        incl = i_refs[g][...]
        bias = jnp.where(lane_p < PAGE_SIZE - 1, pltpu.roll(incl, PAGE_SIZE - 1, axis=1), 0.0) + base
        base = base + incl[:, 0:1]
        kt = k_refs[g][...].reshape(FOX_WIDTH, PAGE_SIZE).astype(BF16)
        s = _dot(qbd, kt)
        scores.append((s.reshape(nh, nq, PAGE_SIZE) + bias[:, None, :]).reshape(rows, PAGE_SIZE))
        values.append(v_refs[g][...].reshape(FOX_WIDTH, PAGE_SIZE).astype(BF16))
    base_ref[...] = base
    update(jnp.concatenate(scores, axis=1), jnp.concatenate(values, axis=1), True)

    @pl.when(j == pl.num_programs(1) - 1)
    def _():
        zpad = jnp.zeros((PAGE_SIZE - nq, FOX_WIDTH), BF16)
        kn = jnp.concatenate([kn_ref[...], zpad], axis=0)
        vn = jnp.concatenate([vn_ref[...], zpad], axis=0)
        s = _dot_nt(qbd, kn).reshape(nh, nq, PAGE_SIZE) - fn_ref[...][:, None, :]
        qpos = lax.broadcasted_iota(jnp.int32, (nh, nq, PAGE_SIZE), 1)
        kpos = lax.broadcasted_iota(jnp.int32, (nh, nq, PAGE_SIZE), 2)
        s = jnp.where(kpos <= qpos, s, NEG_BIG).reshape(rows, PAGE_SIZE)
        update(s, vn, False)
        o = acc_ref[...] / l_ref[...]
        o = jnp.where(own, o, 0.0).reshape(nh, nq, FOX_WIDTH)
        o_ref[...] = jnp.sum(o, axis=0).astype(o_ref.dtype)


def _fox_sample(page_table, q, k_new, v_new, f_new, cache_k, cache_v, suffix):
    b, t, w = q.shape
    n_pages = page_table.shape[1]
    g_pages = PAGES_PER_STEP
    n_steps = n_pages // g_pages

    def page_map(g, nd):
        return lambda bi, j, pt: (pt[bi, n_pages - 1 - (j * g_pages + g)],) + (0,) * nd

    seq = lambda d1, d2: pl.BlockSpec((None, d1, d2), lambda bi, j, pt: (bi, 0, 0))
    in_specs = [seq(t, w), seq(t, w), seq(t, w), seq(FOX_HEADS, PAGE_SIZE)]
    kv_page = (None, FOX_HEADS, FOX_HEAD_DIM, PAGE_SIZE)
    in_specs += [pl.BlockSpec(kv_page, page_map(g, 3)) for g in range(g_pages)]
    in_specs += [pl.BlockSpec(kv_page, page_map(g, 3)) for g in range(g_pages)]
    in_specs += [pl.BlockSpec((None, FOX_HEADS, PAGE_SIZE), page_map(g, 2)) for g in range(g_pages)]
    rows = FOX_HEADS * t
    return pl.pallas_call(
        _fox_sample_kernel,
        grid_spec=pltpu.PrefetchScalarGridSpec(
            num_scalar_prefetch=1,
            grid=(b, n_steps),
            in_specs=in_specs,
            out_specs=seq(t, w),
            scratch_shapes=[pltpu.VMEM((rows, w), BF16), pltpu.VMEM((rows, 1), F32), pltpu.VMEM((rows, 1), F32),
                            pltpu.VMEM((rows, w), F32), pltpu.VMEM((FOX_HEADS, PAGE_SIZE), F32)]),
        out_shape=jax.ShapeDtypeStruct((b, t, w), BF16),
        compiler_params=_params(("arbitrary", "arbitrary")),
        name="fox_sample",
    )(page_table, q, k_new, v_new, f_new, *([cache_k] * g_pages), *([cache_v] * g_pages), *([suffix] * g_pages))


GDN_CHUNK = 128
_HALO = 8


def _unit_lower_inverse(strict_lower, row, col):
    c = strict_lower.shape[0]
    eye = jnp.where(row == col, 1.0, 0.0)
    x = eye - jnp.where((row // 2) == (col // 2), strict_lower, 0.0)
    s = 2
    while s < c:
        off = jnp.where(((row // (2 * s)) == (col // (2 * s))) & ((row // s) != (col // s)), strict_lower, 0.0)
        x = x - _dot3(_dot3(x, off), x)
        s *= 2
    return x


def _gdn_kernel(u_ref, small_ref, smallt_ref, z_ref, s0_ref, conv0_ref, convw_ref, normw_ref,
                o_ref, sout_ref, s_scr, ext_scr, *, per_seq):
    i = pl.program_id(0)
    c = GDN_CHUNK
    hd = GDN_HEAD_DIM

    def load_state():
        s_scr[...] = s0_ref[...]
        ext_scr[0:_HALO, :] = conv0_ref[...]

    if per_seq:
        load_state()
    else:
        pl.when(i == 0)(load_state)

    u = u_ref[...]
    ext_scr[_HALO:_HALO + c, :] = u
    y = u * convw_ref[GDN_CONV - 1:GDN_CONV, :]
    for tap in range(GDN_CONV - 1):
        back = GDN_CONV - 1 - tap
        y = y + ext_scr[_HALO - back:_HALO - back + c, :] * convw_ref[tap:tap + 1, :]
    ext_scr[0:_HALO, :] = u[c - _HALO:c, :]
    y = _silu(y)

    row = lax.broadcasted_iota(jnp.int32, (c, c), 0)
    col = lax.broadcasted_iota(jnp.int32, (c, c), 1)
    incl = row >= col
    small = small_ref[...]
    tril = jnp.where(incl, 1.0, 0.0)
    triu = jnp.where(row <= col, 1.0, 0.0)
    gc_cols = _dot3(tril, small)
    gc_rows = _dot3(smallt_ref[...], triu)

    for h in range(GDN_HEADS):
        sl = slice(h * hd, (h + 1) * hd)
        qc = y[:, sl]
        kc = y[:, GDN_WIDTH + h * hd:GDN_WIDTH + (h + 1) * hd]
        v = y[:, 2 * GDN_WIDTH + h * hd:2 * GDN_WIDTH + (h + 1) * hd]
        q = qc * (lax.rsqrt(jnp.sum(qc * qc, axis=-1, keepdims=True) + NORM_EPS) * hd ** -0.5)
        k = kc * lax.rsqrt(jnp.sum(kc * kc, axis=-1, keepdims=True) + NORM_EPS)
        beta = small[:, 12 + h:13 + h]
        gc = gc_cols[:, 8 + h:9 + h]
        gr = gc_rows[8 + h:9 + h, :]
        g_last = gc[c - 1:c, :]
        decay = jnp.where(incl, jnp.exp(jnp.where(incl, gc - gr, 0.0)), 0.0)
        e_gc = jnp.exp(gc)
        kb = k * beta
        lmat = jnp.where(row > col, _dot3_nt(kb, k) * decay, 0.0)
        tinv = _unit_lower_inverse(lmat, row, col)
        uu = _dot3(tinv, v * beta)
        ww = _dot3(tinv, kb * e_gc)
        a_intra = jnp.where(incl, _dot3_nt(q, k) * decay, 0.0)
        q_dec = q * e_gc
        k_dec = k * jnp.exp(g_last - gc)
        state = s_scr[h]
        v_new = uu - _dot3(ww, state)
        o = _dot3(q_dec, state) + _dot3(a_intra, v_new)
        kh, kl = _split2(k_dec)
        vh, vl = _split2(v_new)
        s_scr[h] = state * jnp.exp(g_last) + (_dot_tn(kh, vh) + (_dot_tn(kh, vl) + _dot_tn(kl, vh)))
        o = o * lax.rsqrt(jnp.mean(o * o, axis=-1, keepdims=True) + NORM_EPS) * normw_ref[...]
        o_ref[:, sl] = (o * _silu(z_ref[:, sl])).astype(o_ref.dtype)

    sout_ref[...] = s_scr[...]


def _gdn(u, small, small_t, z, s0, conv0, conv_w, norm_w, *, per_seq):
    c = GDN_CHUNK
    n = u.shape[0] // c
    nb = s0.shape[0]
    bsel = (lambda i: i) if per_seq else (lambda i: 0)
    return pl.pallas_call(
        functools.partial(_gdn_kernel, per_seq=per_seq),
        grid=(n,),
        in_specs=[pl.BlockSpec((c, 3 * GDN_WIDTH), lambda i: (i, 0)),
                  pl.BlockSpec((c, LANES), lambda i: (i, 0)),
                  pl.BlockSpec((16, c), lambda i: (0, i)),
                  pl.BlockSpec((c, GDN_WIDTH), lambda i: (i, 0)),
                  pl.BlockSpec((None, GDN_HEADS, GDN_HEAD_DIM, GDN_HEAD_DIM), lambda i: (bsel(i), 0, 0, 0)),
                  pl.BlockSpec((None, _HALO, 3 * GDN_WIDTH), lambda i: (bsel(i), 0, 0)),
                  pl.BlockSpec((8, 3 * GDN_WIDTH), lambda i: (0, 0)),
                  pl.BlockSpec((1, GDN_HEAD_DIM), lambda i: (0, 0))],
        out_specs=[pl.BlockSpec((c, GDN_WIDTH), lambda i: (i, 0)),
                   pl.BlockSpec((None, GDN_HEADS, GDN_HEAD_DIM, GDN_HEAD_DIM), lambda i: (bsel(i), 0, 0, 0))],
        out_shape=[jax.ShapeDtypeStruct((n * c, GDN_WIDTH), BF16),
                   jax.ShapeDtypeStruct((nb, GDN_HEADS, GDN_HEAD_DIM, GDN_HEAD_DIM), F32)],
        scratch_shapes=[pltpu.VMEM((GDN_HEADS, GDN_HEAD_DIM, GDN_HEAD_DIM), F32),
                        pltpu.VMEM((_HALO + c, 3 * GDN_WIDTH), F32)],
        compiler_params=_params(("arbitrary",)),
        name="gdn",
    )(u, small, small_t, z, s0, conv0, conv_w, norm_w)


_GROUP_LANE0 = N_EXPERTS


def _layer_norm(y, g, b):
    mu = jnp.mean(y, axis=-1, keepdims=True)
    d = y - mu
    var = jnp.mean(d * d, axis=-1, keepdims=True)
    return d * lax.rsqrt(var + LN_EPS) * g + b


def _pack_router(w_grp, b_grp, w_rt, b_rt):
    d = w_grp.shape[0]
    w = jnp.concatenate([w_rt, w_grp, jnp.zeros((d, LANES - N_EXPERTS - N_GROUPS), F32)], axis=1)
    b = jnp.concatenate([b_rt, b_grp, jnp.zeros((LANES - N_EXPERTS - N_GROUPS,), F32)]).reshape(1, LANES)
    wh = w.astype(BF16)
    wl = (w - wh.astype(F32)).astype(BF16)
    return wh, wl, b


def _mix_kernel(x_ref, of_ref, og_ref, gf_ref, gg_ref, wuf_ref, wug_ref, wo_ref, gate_ref, lng_ref, lnb_ref,
                shift_ref, scale_ref, wrh_ref, wrl_ref, br_ref,
                x1_ref, h2_ref, route_ref, sel_ref):
    merged = (gf_ref[...].astype(F32) * _dot(of_ref[...], wuf_ref[...])
              + gg_ref[...].astype(F32) * _dot(og_ref[...], wug_ref[...]))
    mix = _dot(merged.astype(BF16), wo_ref[...])
    x1 = _layer_norm(DEEPNORM_ALPHA * x_ref[...] + (1.0 + gate_ref[...]) * mix, lng_ref[...], lnb_ref[...])
    x1_ref[...] = x1
    h2 = x1 * (1.0 + scale_ref[...]) + shift_ref[...]
    h2_ref[...] = h2

    hh, hl = _split2(h2)
    wrh = wrh_ref[...]
    logits = _dot(hh, wrh) + (_dot(hh, wrl_ref[...]) + _dot(hl, wrh)) + br_ref[...]
    lane = lax.broadcasted_iota(jnp.int32, logits.shape, 1)
    big = jnp.int32(LANES)
    neg = jnp.float32(-jnp.inf)

    def top1(vals):
        m = jnp.max(vals, axis=-1, keepdims=True)
        idx = jnp.min(jnp.where(vals == m, lane, big), axis=-1, keepdims=True)
        return m, idx

    is_grp = (lane >= _GROUP_LANE0) & (lane < _GROUP_LANE0 + N_GROUPS)
    glog = jnp.where(is_grp, logits, neg)
    gmax, gidx = top1(glog)
    grp_p = 1.0 / jnp.sum(jnp.exp(glog - gmax), axis=-1, keepdims=True)
    grp = gidx - _GROUP_LANE0
    elog = jnp.where((lane // EXPERTS_PER_GROUP) == grp, logits, neg)
    m1, i1 = top1(elog)
    m2, i2 = top1(jnp.where(lane == i1, neg, elog))
    e21 = jnp.exp(m2 - m1)
    w1 = grp_p / (1.0 + e21)
    w2 = grp_p * e21 / (1.0 + e21)
    route = jnp.where(lane == 0, i1.astype(F32),
                      jnp.where(lane == 1, i2.astype(F32),
                                jnp.where(lane == 2, w1, jnp.where(lane == 3, w2, 0.0))))
    route_ref[...] = route
    sel_ref[...] = jnp.where((lane == i1) | (lane == i2), 1.0, 0.0).astype(BF16)


def _mix(x, o_fox, o_gdn, gf, gg, wuf, wug, wo, gate, ln_g, ln_b, shift, scale, wrh, wrl, br, *, tm):
    t, d = x.shape
    per_tok = gate.shape[0] != 1
    mod_spec = (pl.BlockSpec((tm, d), lambda i: (i, 0)) if per_tok
                else pl.BlockSpec((1, d), lambda i: (0, 0)))
    const = lambda a: pl.BlockSpec(a.shape, lambda i: (0, 0))
    tile = lambda n: pl.BlockSpec((tm, n), lambda i: (i, 0))
    return pl.pallas_call(
        _mix_kernel,
        grid=(t // tm,),
        in_specs=[tile(d), tile(FOX_WIDTH), tile(GDN_WIDTH), tile(d), tile(d), const(wuf), const(wug), const(wo),
                  mod_spec, const(ln_g), const(ln_b), mod_spec, mod_spec, const(wrh), const(wrl), const(br)],
        out_specs=[tile(d), tile(d), tile(LANES), tile(LANES)],
        out_shape=[jax.ShapeDtypeStruct((t, d), F32), jax.ShapeDtypeStruct((t, d), F32),
                   jax.ShapeDtypeStruct((t, LANES), F32), jax.ShapeDtypeStruct((t, LANES), BF16)],
        compiler_params=_params(("arbitrary",)),
        name="mix_ln_route",
    )(x, o_fox, o_gdn, gf, gg, wuf, wug, wo, gate, ln_g, ln_b, shift, scale, wrh, wrl, br)


MOE_ROWS = 256


def _rank_kernel(sel_ref, route_ref, out_ref, cnt_ref, carry_ref):
    i = pl.program_id(0)
    tm = sel_ref.shape[0]

    @pl.when(i == 0)
    def _():
        carry_ref[...] = jnp.zeros_like(carry_ref)

    sel = sel_ref[...]
    row = lax.broadcasted_iota(jnp.int32, (tm, tm), 0)
    col = lax.broadcasted_iota(jnp.int32, (tm, tm), 1)
    strict = jnp.where(row > col, 1.0, 0.0).astype(BF16)
    rank = _dot(strict, sel) + carry_ref[0:1, :]
    lane = lax.broadcasted_iota(jnp.int32, rank.shape, 1)
    route = route_ref[...]
    e0 = route[:, 0:1].astype(jnp.int32)
    e1 = route[:, 1:2].astype(jnp.int32)
    r0 = jnp.sum(jnp.where(lane == e0, rank, 0.0), axis=-1, keepdims=True)
    r1 = jnp.sum(jnp.where(lane == e1, rank, 0.0), axis=-1, keepdims=True)
    out_ref[...] = jnp.where(lane == 0, r0, jnp.where(lane == 1, r1, 0.0))
    total = rank[tm - 1:tm, :] + sel[tm - 1:tm, :].astype(F32)
    carry_ref[0:1, :] = total
    cnt_ref[...] = jnp.broadcast_to(total, cnt_ref.shape)


def _rank(sel, route, *, tm):
    t = sel.shape[0]
    tile = pl.BlockSpec((tm, LANES), lambda i: (i, 0))
    return pl.pallas_call(
        _rank_kernel,
        grid=(t // tm,),
        in_specs=[tile, tile],
        out_specs=[tile, pl.BlockSpec((8, LANES), lambda i: (0, 0))],
        out_shape=[jax.ShapeDtypeStruct((t, LANES), F32), jax.ShapeDtypeStruct((8, LANES), F32)],
        scratch_shapes=[pltpu.VMEM((8, LANES), F32)],
        compiler_params=_params(("arbitrary",)),
        name="moe_rank",
    )(sel, route)


def _scatter_kernel(pos_ref, h_ref, init_ref, xs_ref, sem, *, tok0, n_tok):
    del init_ref
    i = pl.program_id(0)
    tm = h_ref.shape[0]

    def copy(r, slot):
        dst = pos_ref[slot * n_tok + tok0 + i * tm + r]
        return pltpu.make_async_copy(h_ref.at[pl.ds(r, 1), :], xs_ref.at[pl.ds(dst, 1), :], sem)

    def start(r, c):
        copy(r, 0).start()
        copy(r, 1).start()
        return c

    def wait(r, c):
        copy(r, 0).wait()
        copy(r, 1).wait()
        return c

    lax.fori_loop(0, tm, start, 0, unroll=8)
    lax.fori_loop(0, tm, wait, 0, unroll=8)


def _scatter_rows(pos, h, init, *, tm, tok0):
    t, d = h.shape
    n_sorted = init.shape[0]
    return pl.pallas_call(
        functools.partial(_scatter_kernel, tok0=tok0, n_tok=pos.shape[0] // 2),
        grid_spec=pltpu.PrefetchScalarGridSpec(
            num_scalar_prefetch=1,
            grid=(t // tm,),
            in_specs=[pl.BlockSpec((tm, d), lambda i, pos: (i, 0)),
                      pl.BlockSpec(memory_space=pl.ANY)],
            out_specs=pl.BlockSpec(memory_space=pl.ANY),
            scratch_shapes=[pltpu.SemaphoreType.DMA(())]),
        out_shape=jax.ShapeDtypeStruct((n_sorted, d), F32),
        input_output_aliases={2: 0},
        compiler_params=_params(("arbitrary",)),
        name="moe_scatter",
    )(pos, h, init)


def _experts_kernel(te_ref, x_ref, wg_ref, wu_ref, wd_ref, y_ref):
    i = pl.program_id(0)

    @pl.when(te_ref[1, i] > 0)
    def _():
        xb = x_ref[...].astype(BF16)
        act = _silu(_dot(xb, wg_ref[...].astype(BF16))) * _dot(xb, wu_ref[...].astype(BF16))
        y_ref[...] = _dot(act.astype(BF16), wd_ref[...].astype(BF16))

    @pl.when(te_ref[1, i] == 0)
    def _():
        y_ref[...] = jnp.zeros_like(y_ref)


def _experts(tile_info, xs, w_g, w_u, w_d):
    n_sorted, d = xs.shape
    ff = w_g.shape[2]
    n_tiles = n_sorted // MOE_ROWS
    return pl.pallas_call(
        _experts_kernel,
        grid_spec=pltpu.PrefetchScalarGridSpec(
            num_scalar_prefetch=1,
            grid=(n_tiles,),
            in_specs=[pl.BlockSpec((MOE_ROWS, d), lambda i, te: (i, 0)),
                      pl.BlockSpec((None, d, ff), lambda i, te: (te[0, i], 0, 0)),
                      pl.BlockSpec((None, d, ff), lambda i, te: (te[0, i], 0, 0)),
                      pl.BlockSpec((None, ff, d), lambda i, te: (te[0, i], 0, 0))],
            out_specs=pl.BlockSpec((MOE_ROWS, d), lambda i, te: (i, 0))),
        out_shape=jax.ShapeDtypeStruct((n_sorted, d), F32),
        compiler_params=_params(("arbitrary",)),
        name="moe_experts",
    )(tile_info, xs, w_g, w_u, w_d)


def _combine_kernel(pos_ref, ys_ref, x1_ref, route_ref, gate_ref, lng_ref, lnb_ref, out_ref, buf, sem,
                    *, tok0, n_tok):
    i = pl.program_id(0)
    tm = x1_ref.shape[0]

    def copy(r, slot):
        src = pos_ref[slot * n_tok + tok0 + i * tm + r]
        return pltpu.make_async_copy(ys_ref.at[pl.ds(src, 1), :], buf.at[slot, pl.ds(r, 1), :], sem)

    def start(r, c):
        copy(r, 0).start()
        copy(r, 1).start()
        return c

    def wait(r, c):
        copy(r, 0).wait()
        copy(r, 1).wait()
        return c

    lax.fori_loop(0, tm, start, 0, unroll=8)
    lax.fori_loop(0, tm, wait, 0, unroll=8)
    route = route_ref[...]
    ffn = route[:, 2:3] * buf[0] + route[:, 3:4] * buf[1]
    y = DEEPNORM_ALPHA * x1_ref[...] + (1.0 + gate_ref[...]) * ffn
    out_ref[...] = _layer_norm(y, lng_ref[...], lnb_ref[...])


def _combine(pos, ys, x1, route, gate, ln_g, ln_b, *, tm, tok0):
    t, d = x1.shape
    gate_spec = (pl.BlockSpec((tm, d), lambda i, pos: (i, 0)) if gate.shape[0] != 1
                 else pl.BlockSpec((1, d), lambda i, pos: (0, 0)))
    return pl.pallas_call(
        functools.partial(_combine_kernel, tok0=tok0, n_tok=pos.shape[0] // 2),
        grid_spec=pltpu.PrefetchScalarGridSpec(
            num_scalar_prefetch=1,
            grid=(t // tm,),
            in_specs=[pl.BlockSpec(memory_space=pl.ANY),
                      pl.BlockSpec((tm, d), lambda i, pos: (i, 0)),
                      pl.BlockSpec((tm, LANES), lambda i, pos: (i, 0)),
                      gate_spec,
                      pl.BlockSpec((1, d), lambda i, pos: (0, 0)),
                      pl.BlockSpec((1, d), lambda i, pos: (0, 0))],
            out_specs=pl.BlockSpec((tm, d), lambda i, pos: (i, 0)),
            scratch_shapes=[pltpu.VMEM((2, tm, d), F32), pltpu.SemaphoreType.DMA(())]),
        out_shape=jax.ShapeDtypeStruct((t, d), F32),
        compiler_params=_params(("arbitrary",)),
        name="moe_combine_ln",
    )(pos, ys, x1, route, gate, ln_g, ln_b)


def _moe(groups, w_g, w_u, w_d, ln_g, ln_b, *, tm):
    route = jnp.concatenate([g[2] for g in groups], axis=0)
    sel = jnp.concatenate([g[3] for g in groups], axis=0)
    n = route.shape[0]
    ranks, counts = _rank(sel, route, tm=tm)
    counts = counts[0, :N_EXPERTS].astype(jnp.int32)
    padded = ((counts + MOE_ROWS - 1) // MOE_ROWS) * MOE_ROWS
    ends = jnp.cumsum(padded)
    offsets = ends - padded
    n_tiles = (2 * n) // MOE_ROWS + N_EXPERTS
    tile_start = jnp.arange(n_tiles, dtype=jnp.int32) * MOE_ROWS
    tile_expert = jnp.sum((tile_start[:, None] >= ends[None, :]).astype(jnp.int32), axis=1)
    tile_expert = jnp.minimum(tile_expert, N_EXPERTS - 1)
    tile_used = (tile_start < ends[-1]).astype(jnp.int32)
    tile_info = jnp.stack([tile_expert, tile_used])
    e = route[:, 0:2].astype(jnp.int32)
    r = ranks[:, 0:2].astype(jnp.int32)
    pos = (offsets[e] + r).T.reshape(-1)
    xs = jnp.zeros((n_tiles * MOE_ROWS, h2_dim(groups)), F32)
    tok0 = 0
    for h2, _, _, _, _ in groups:
        xs = _scatter_rows(pos, h2, xs, tm=tm, tok0=tok0)
        tok0 += h2.shape[0]
    ys = _experts(tile_info, xs, w_g, w_u, w_d)
    outs = []
    tok0 = 0
    for _, x1, rt, _, gate in groups:
        outs.append(_combine(pos, ys, x1, rt, gate, ln_g, ln_b, tm=tm, tok0=tok0))
        tok0 += x1.shape[0]
    return outs


def h2_dim(groups):
    return groups[0][0].shape[1]


TOKEN_TILE = 256
ATTN_TILE = 512


def kernel(x_prompt, x_sample, cache_fox_k, cache_fox_v, cache_fox_logf, state_gdn, state_gdn_conv, page_table,
           c_prompt, c_sample, w_ada_mix, b_ada_mix, w_in, b_forget, gdn_conv_w, gdn_a_log, gdn_dt_bias,
           gdn_norm_w, w_up_fox, w_up_gdn, w_out, ln1_g, ln1_b, w_ada_ffn, b_ada_ffn, w_group_router,
           b_group_router, w_expert_router, b_expert_router, w_expert_gate, w_expert_up, w_expert_down,
           ln2_g, ln2_b):
    assert w_in.shape[0] == 1, "single-layer step"
    bp, tp, d = x_prompt.shape
    bs, ts, _ = x_sample.shape
    assert bp == 1
    tm = TOKEN_TILE
    row = lambda a: a.reshape(1, -1)

    n_c = bp + bs
    c_all = jnp.concatenate([c_prompt, c_sample, jnp.zeros((-n_c % 8, d), F32)], axis=0)
    mod_mix = _adaln(c_all, w_ada_mix[0], b_ada_mix[0])
    mod_ffn = _adaln(c_all, w_ada_ffn[0], b_ada_ffn[0])

    def split_mod(mod, lo, hi, rep):
        m = mod[lo:hi]
        if rep > 1:
            m = jnp.repeat(m, rep, axis=0)
        return m[:, 0:d], m[:, d:2 * d], m[:, 2 * d:3 * d]

    wide, wsh, wsl = _pack_w_in(jnp.transpose(w_in[0]))
    prow, pcol = _param_rows(b_forget[0], gdn_a_log[0], gdn_dt_bias[0])
    conv_w = jnp.pad(gdn_conv_w[0], ((0, 8 - GDN_CONV), (0, 0)))
    norm_w = row(gdn_norm_w[0])
    wuf, wug, wo = w_up_fox[0].astype(BF16), w_up_gdn[0].astype(BF16), w_out[0].astype(BF16)
    wrh, wrl, br = _pack_router(w_group_router[0], b_group_router[0], w_expert_router[0], b_expert_router[0])
    ln1 = (row(ln1_g[0]), row(ln1_b[0]))
    heads = (FOX_HEADS, FOX_HEAD_DIM)

    shift1, scale1, gate1 = split_mod(mod_mix, 0, 1, 1)
    shift2, scale2, gate2_p = split_mod(mod_ffn, 0, 1, 1)
    xp = x_prompt[0]
    pr = _project(xp, shift1, scale1, wide, wsh, wsl, prow, pcol, tm=ATTN_TILE, seg_len=None, kv_transposed=True)
    o_fox = _fox_prompt(pr['q'], pr['ktb'], pr['vb'], pr['ft'][0:FOX_HEADS], tq=ATTN_TILE)
    o_gdn, state_p = _gdn(pr['gqkv'], pr['small'], pr['small_t'], pr['gz'],
                          jnp.zeros((1, GDN_HEADS, GDN_HEAD_DIM, GDN_HEAD_DIM), F32),
                          jnp.zeros((1, _HALO, 3 * GDN_WIDTH), F32), conv_w, norm_w, per_seq=False)
    x1_p, h2_p, route_p, sel_p = _mix(xp, o_fox, o_gdn, pr['gf'], pr['gg'], wuf, wug, wo, gate1, *ln1,
                                      shift2, scale2, wrh, wrl, br, tm=tm)
    head_major = lambda a: jnp.transpose(a.reshape(*heads, tp), (2, 0, 1))
    prompt_rows = (head_major(pr['ktf']), head_major(pr['vtf']), jnp.transpose(pr['small_t'][0:FOX_HEADS]),
                   state_p, pr['gqkv'][tp - (GDN_CONV - 1):])

    n_s = bs * ts
    shift1, scale1, gate1 = split_mod(mod_mix, 1, 1 + bs, ts)
    shift2, scale2, gate2_s = split_mod(mod_ffn, 1, 1 + bs, ts)
    xs = x_sample.reshape(n_s, d)
    sr = _project(xs, shift1, scale1, wide, wsh, wsl, prow, pcol, tm=n_s, seg_len=ts, kv_transposed=False)
    suffix = _page_suffix(jnp.transpose(cache_fox_logf[0], (0, 2, 1)))
    lanes_per_seq = lambda a: a.reshape(a.shape[0], bs, ts)
    f_new = jnp.pad(jnp.transpose(lanes_per_seq(sr['ft'][0:FOX_HEADS]), (1, 0, 2)),
                    ((0, 0), (0, 0), (0, PAGE_SIZE - ts)))
    seq3 = lambda a: a.reshape(bs, ts, a.shape[-1])
    o_fox = _fox_sample(page_table, seq3(sr['q']), seq3(sr['kb']), seq3(sr['vb']), f_new,
                        jnp.transpose(cache_fox_k[0], (0, 2, 3, 1)), jnp.transpose(cache_fox_v[0], (0, 2, 3, 1)),
                        suffix).reshape(n_s, FOX_WIDTH)
    chunk = lambda a: jnp.pad(seq3(a), ((0, 0), (0, GDN_CHUNK - ts), (0, 0))).reshape(bs * GDN_CHUNK, a.shape[-1])
    small_t_c = jnp.pad(lanes_per_seq(sr['small_t']), ((0, 0), (0, 0), (0, GDN_CHUNK - ts))).reshape(N_SMALL, -1)
    conv0 = jnp.pad(state_gdn_conv[0], ((0, 0), (_HALO - (GDN_CONV - 1), 0), (0, 0)))
    o_gdn, state_s = _gdn(chunk(sr['gqkv']), chunk(sr['small']), small_t_c, chunk(sr['gz']), state_gdn[0], conv0,
                          conv_w, norm_w, per_seq=True)
    o_gdn = o_gdn.reshape(bs, GDN_CHUNK, GDN_WIDTH)[:, 0:ts].reshape(n_s, GDN_WIDTH)
    x1_s, h2_s, route_s, sel_s = _mix(xs, o_fox, o_gdn, sr['gf'], sr['gg'], wuf, wug, wo, gate1, *ln1,
                                      shift2, scale2, wrh, wrl, br, tm=n_s)
    sample_rows = (sr['kf'], sr['vf'], sr['small'][:, 0:FOX_HEADS], state_s,
                   seq3(sr['gqkv'])[:, ts - (GDN_CONV - 1):])

    y_p, y_s = _moe([(h2_p, x1_p, route_p, sel_p, gate2_p), (h2_s, x1_s, route_s, sel_s, gate2_s)],
                    w_expert_gate[0], w_expert_up[0], w_expert_down[0], row(ln2_g[0]), row(ln2_b[0]), tm=tm)

    kp, vp, lfp, sp, cvp = prompt_rows
    ks, vs, lfs, ss, cvs = sample_rows
    return (y_p.reshape(1, tp, d), y_s.reshape(bs, ts, d),
            kp.reshape(1, 1, tp, *heads), vp.reshape(1, 1, tp, *heads), lfp.reshape(1, 1, tp, FOX_HEADS),
            sp.reshape(1, 1, GDN_HEADS, GDN_HEAD_DIM, GDN_HEAD_DIM), cvp.reshape(1, 1, GDN_CONV - 1, 3 * GDN_WIDTH),
            ks.reshape(1, bs, ts, *heads), vs.reshape(1, bs, ts, *heads), lfs.reshape(1, bs, ts, FOX_HEADS),
            ss.reshape(1, bs, GDN_HEADS, GDN_HEAD_DIM, GDN_HEAD_DIM),
            cvs.reshape(1, bs, GDN_CONV - 1, 3 * GDN_WIDTH))
```

```python
import functools

import jax
import jax.numpy as jnp
from jax import lax
from jax.experimental import pallas as pl
from jax.experimental.pallas import tpu as pltpu

F32 = jnp.float32
BF16 = jnp.bfloat16

D_MODEL = 1024
FOX_HEADS = 8
FOX_HEAD_DIM = 64
FOX_WIDTH = FOX_HEADS * FOX_HEAD_DIM
GDN_HEADS = 4
GDN_HEAD_DIM = 128
GDN_WIDTH = GDN_HEADS * GDN_HEAD_DIM
GDN_CONV = 4
N_GROUPS = 4
EXPERTS_PER_GROUP = 8
N_EXPERTS = N_GROUPS * EXPERTS_PER_GROUP
EXPERT_FF = 512
PAGE_SIZE = 128
DEEPNORM_ALPHA = 2.0 ** 0.25
LN_EPS = 1e-5
NORM_EPS = 1e-6
LOG2E = 1.4426950408889634
LANES = 128
VMEM_LIMIT = 56 * 1024 * 1024

_C_FQ, _C_FK, _C_FV, _C_GQKV, _C_GZ, _C_GF, _C_GG, _C_END = 0, 512, 1024, 1536, 3072, 3584, 4608, 5632


def _params(sem):
    return pltpu.CompilerParams(dimension_semantics=sem, vmem_limit_bytes=VMEM_LIMIT)


def _dot(a, b):
    return jnp.dot(a, b, preferred_element_type=F32)


def _dot_nt(a, b):
    return lax.dot_general(a, b, (((1,), (1,)), ((), ())), preferred_element_type=F32)


def _dot_tn(a, b):
    return lax.dot_general(a, b, (((0,), (0,)), ((), ())), preferred_element_type=F32)


def _split2(x):
    hi = x.astype(BF16)
    lo = (x - hi.astype(F32)).astype(BF16)
    return hi, lo


def _split3(x):
    a = x.astype(BF16)
    r = x - a.astype(F32)
    b = r.astype(BF16)
    c = (r - b.astype(F32)).astype(BF16)
    return a, b, c


def _dot3(a, b):
    ah, al = _split2(a)
    bh, bl = _split2(b)
    return _dot(ah, bh) + (_dot(ah, bl) + _dot(al, bh))


def _dot3_nt(a, b):
    ah, al = _split2(a)
    bh, bl = _split2(b)
    return _dot_nt(ah, bh) + (_dot_nt(ah, bl) + _dot_nt(al, bh))


def _silu(x):
    return x * jax.nn.sigmoid(x)


def _softplus(x):
    return jnp.maximum(x, 0.0) + jnp.log1p(jnp.exp(-jnp.abs(x)))


def _log_sigmoid(x):
    return jnp.minimum(x, 0.0) - jnp.log1p(jnp.exp(-jnp.abs(x)))


def _ada_kernel(c_ref, w_ref, b_ref, o_ref):
    s = _silu(c_ref[...])
    o_ref[...] = jnp.dot(s, w_ref[...], preferred_element_type=F32,
                         precision=lax.Precision.HIGHEST) + b_ref[...]


def _adaln(c, w, b):
    rows, d = c.shape
    n = w.shape[1]
    bn = 512
    return pl.pallas_call(
        _ada_kernel,
        grid=(n // bn,),
        in_specs=[pl.BlockSpec((rows, d), lambda j: (0, 0)),
                  pl.BlockSpec((d, bn), lambda j: (0, j)),
                  pl.BlockSpec((1, bn), lambda j: (0, j))],
        out_specs=pl.BlockSpec((rows, bn), lambda j: (0, j)),
        out_shape=jax.ShapeDtypeStruct((rows, n), F32),
        compiler_params=_params(("arbitrary",)),
        name="adaln",
    )(c, w, b.reshape(1, n))


KEY_TILE = 256
N_SMALL = 16


def _pack_w_in(w_in_t):
    wide = jnp.concatenate([w_in_t[0:1536], w_in_t[1544:3592], w_in_t[3600:5648]], axis=0).astype(BF16)
    small = jnp.concatenate([w_in_t[1536:1544], w_in_t[3592:3600],
                             jnp.zeros((LANES - N_SMALL, w_in_t.shape[1]), F32)], axis=0)
    sh = small.astype(BF16)
    sl = (small - sh.astype(F32)).astype(BF16)
    return wide, sh, sl


def _param_rows(b_forget, a_log, dt_bias):
    z = jnp.zeros((LANES,), F32)
    bias = z.at[0:8].set(b_forget).at[8:12].set(dt_bias)
    nega = z.at[8:12].set(-jnp.exp(a_log))
    rows = jnp.zeros((8, LANES), F32).at[0].set(bias).at[1].set(nega)
    cols = jnp.zeros((N_SMALL, LANES), F32).at[:, 0].set(bias[0:N_SMALL]).at[:, 1].set(nega[0:N_SMALL])
    return rows, cols


def _small_act(z, idx, nega):
    return jnp.where(idx < 8, _log_sigmoid(z), jnp.where(idx < 12, nega * _softplus(z), jax.nn.sigmoid(z)))


def _proj_kernel(x_ref, shift_ref, scale_ref, w_ref, wsh_ref, wsl_ref, prow_ref, pcol_ref, *refs,
                 seg_len, kv_transposed):
    if kv_transposed:
        (q_ref, vb_ref, gqkv_ref, gz_ref, gf_ref, gg_ref, small_ref, smallt_ref, ft_ref,
         ktf_ref, ktb_ref, vtf_ref, carry_ref) = refs
    else:
        (q_ref, vb_ref, gqkv_ref, gz_ref, gf_ref, gg_ref, small_ref, smallt_ref, ft_ref,
         kf_ref, kb_ref, vf_ref, carry_ref) = refs
    i = pl.program_id(0)
    tm = x_ref.shape[0]

    @pl.when(i == 0)
    def _():
        carry_ref[...] = jnp.zeros_like(carry_ref)

    h = x_ref[...] * (1.0 + scale_ref[...]) + shift_ref[...]
    hb, hl = _split2(h)
    seg = lambda lo, hi: w_ref[lo:hi, :]

    q_ref[...] = (_dot_nt(hb, seg(_C_FQ, _C_FK)) * (LOG2E * FOX_HEAD_DIM ** -0.5)).astype(BF16)
    v = _dot_nt(hb, seg(_C_FV, _C_GQKV))
    vb_ref[...] = v.astype(BF16)
    if kv_transposed:
        kt = _dot_nt(seg(_C_FK, _C_FV), hb)
        ktf_ref[...] = kt
        ktbf = kt.astype(BF16)
        for c in range(ktb_ref.shape[0]):
            ktb_ref[c] = ktbf[:, c * KEY_TILE:(c + 1) * KEY_TILE]
        vtf_ref[...] = _dot_nt(seg(_C_FV, _C_GQKV), hb)
    else:
        k = _dot_nt(hb, seg(_C_FK, _C_FV))
        kf_ref[...] = k
        kb_ref[...] = k.astype(BF16)
        vf_ref[...] = v
    gqkv_ref[...] = _dot_nt(hb, seg(_C_GQKV, _C_GZ))
    gz_ref[...] = _dot_nt(hb, seg(_C_GZ, _C_GF))
    gf_ref[...] = jax.nn.sigmoid(_dot_nt(hb, seg(_C_GF, _C_GG))).astype(BF16)
    gg_ref[...] = jax.nn.sigmoid(_dot_nt(hb, seg(_C_GG, _C_END))).astype(BF16)

    wsh = wsh_ref[...]
    z = _dot_nt(hb, wsh) + (_dot_nt(hb, wsl_ref[...]) + _dot_nt(hl, wsh)) + prow_ref[0:1, :]
    lane = lax.broadcasted_iota(jnp.int32, z.shape, 1)
    small_ref[...] = _small_act(z, lane, prow_ref[1:2, :])

    ws16h = wsh_ref[0:N_SMALL, :]
    zt = (_dot_nt(ws16h, hb) + (_dot_nt(wsl_ref[0:N_SMALL, :], hb) + _dot_nt(ws16h, hl))) + pcol_ref[:, 0:1]
    rid = lax.broadcasted_iota(jnp.int32, zt.shape, 0)
    small_t = _small_act(zt, rid, pcol_ref[:, 1:2])
    smallt_ref[...] = small_t
    row = lax.broadcasted_iota(jnp.int32, (tm, tm), 0)
    col = lax.broadcasted_iota(jnp.int32, (tm, tm), 1)
    keep = row <= col
    if seg_len is not None:
        keep = keep & ((row // seg_len) == (col // seg_len))
    tri = jnp.where(keep, 1.0, 0.0).astype(BF16)
    l0, l1, l2 = _split3(jnp.where(rid < FOX_HEADS, small_t, 0.0))
    fcum = _dot(l0, tri) + (_dot(l1, tri) + _dot(l2, tri))
    if seg_len is None:
        fcum = fcum + carry_ref[:, 0:1]
        carry_ref[...] = jnp.broadcast_to(fcum[:, tm - 1:tm], carry_ref.shape)
    ft_ref[...] = fcum * LOG2E


def _project(x, shift, scale, wide, wsh, wsl, prow, pcol, *, tm, seg_len, kv_transposed):
    t, d = x.shape
    nt = t // tm
    per_tok = shift.shape[0] != 1
    mod_spec = (pl.BlockSpec((tm, d), lambda i: (i, 0)) if per_tok
                else pl.BlockSpec((1, d), lambda i: (0, 0)))
    const = lambda a: pl.BlockSpec(a.shape, lambda i: (0, 0))
    tile = lambda n: pl.BlockSpec((tm, n), lambda i: (i, 0))
    ttile = lambda n: pl.BlockSpec((n, tm), lambda i: (0, i))
    tok = lambda n, dt: (tile(n), jax.ShapeDtypeStruct((t, n), dt))
    tra = lambda n, dt: (ttile(n), jax.ShapeDtypeStruct((n, t), dt))
    outs = {'q': tok(FOX_WIDTH, BF16), 'vb': tok(FOX_WIDTH, BF16), 'gqkv': tok(3 * GDN_WIDTH, F32),
            'gz': tok(GDN_WIDTH, F32), 'gf': tok(D_MODEL, BF16), 'gg': tok(D_MODEL, BF16),
            'small': tok(LANES, F32), 'small_t': tra(N_SMALL, F32), 'ft': tra(N_SMALL, F32)}
    if kv_transposed:
        outs['ktf'] = tra(FOX_WIDTH, F32)
        outs['ktb'] = (pl.BlockSpec((tm // KEY_TILE, FOX_WIDTH, KEY_TILE), lambda i: (i, 0, 0)),
                       jax.ShapeDtypeStruct((t // KEY_TILE, FOX_WIDTH, KEY_TILE), BF16))
        outs['vtf'] = tra(FOX_WIDTH, F32)
    else:
        outs['kf'] = tok(FOX_WIDTH, F32)
        outs['kb'] = tok(FOX_WIDTH, BF16)
        outs['vf'] = tok(FOX_WIDTH, F32)
    res = pl.pallas_call(
        functools.partial(_proj_kernel, seg_len=seg_len, kv_transposed=kv_transposed),
        grid=(nt,),
        in_specs=[tile(d), mod_spec, mod_spec, const(wide), const(wsh), const(wsl), const(prow), const(pcol)],
        out_specs=[v[0] for v in outs.values()],
        out_shape=[v[1] for v in outs.values()],
        scratch_shapes=[pltpu.VMEM((N_SMALL, LANES), F32)],
        compiler_params=_params(("arbitrary",)),
        name="in_proj",
    )(x, shift, scale, wide, wsh, wsl, prow, pcol)
    return dict(zip(outs.keys(), res))


NEG_BIG = -1e30


def _fox_prompt_kernel(q_ref, kt_ref, v_ref, fk_ref, o_ref, qs_ref, m_ref, acc_ref, *, tk):
    qi = pl.program_id(1)
    tq = q_ref.shape[0]
    q = q_ref[...]
    lane = lax.broadcasted_iota(jnp.int32, q.shape, 1)
    lane_k = lax.broadcasted_iota(jnp.int32, (tk, LANES), 1)
    n_diag = tq // tk
    row = lax.broadcasted_iota(jnp.int32, (tq, tk), 0)
    col = lax.broadcasted_iota(jnp.int32, (tq, tk), 1)

    qs_ref[0:tq, :] = jnp.where(lane < FOX_HEAD_DIM, q, jnp.zeros_like(q))
    qs_ref[tq:2 * tq, :] = jnp.where(lane >= FOX_HEAD_DIM, q, jnp.zeros_like(q))
    m_ref[...] = jnp.full(m_ref.shape, NEG_BIG, F32)
    acc_ref[...] = jnp.zeros(acc_ref.shape, F32)

    def tile(j, masked_from):
        off = pl.multiple_of(j * tk, tk)
        v = v_ref[pl.ds(off, tk), :]
        s_both = _dot(qs_ref[...], kt_ref[j])
        for h in range(2):
            s = s_both[h * tq:(h + 1) * tq] - fk_ref[h, pl.ds(j, 1), :]
            if masked_from is not None:
                s = jnp.where(col + masked_from * tk <= row, s, NEG_BIG)
            vh = jnp.where((lane_k < FOX_HEAD_DIM) == (h == 0), v, jnp.ones_like(v))
            m_prev = m_ref[h]
            m_new = jnp.maximum(m_prev, jnp.max(s, axis=-1, keepdims=True))
            alpha = jnp.exp2(m_prev - m_new)
            p = jnp.concatenate([jnp.exp2(s[:, c * LANES:(c + 1) * LANES] - m_new)
                                 for c in range(tk // LANES)], axis=1).astype(BF16)
            acc_ref[h] = alpha * acc_ref[h] + _dot(p, vh)
            m_ref[h] = m_new

    def body(jj, c):
        for u in range(n_diag):
            tile(jj * n_diag + u, None)
        return c

    lax.fori_loop(0, qi, body, 0)
    for u in range(n_diag):
        tile(qi * n_diag + u, u)

    a0, a1 = acc_ref[0], acc_ref[1]
    o0 = a0 / pltpu.roll(a0, FOX_HEAD_DIM, axis=1)
    o1 = a1 / pltpu.roll(a1, FOX_HEAD_DIM, axis=1)
    o_ref[...] = jnp.where(lane < FOX_HEAD_DIM, o0, o1).astype(o_ref.dtype)


def _fox_prompt(q, kt, v, f_rows, *, tq):
    t = q.shape[0]
    nk, _, tk = kt.shape
    n_pairs = FOX_WIDTH // LANES
    fk = f_rows.reshape(n_pairs, 2, nk, tk)
    return pl.pallas_call(
        functools.partial(_fox_prompt_kernel, tk=tk),
        grid=(n_pairs, t // tq),
        in_specs=[pl.BlockSpec((tq, LANES), lambda p, i: (i, p)),
                  pl.BlockSpec((nk, LANES, tk), lambda p, i: (0, p, 0)),
                  pl.BlockSpec((t, LANES), lambda p, i: (0, p)),
                  pl.BlockSpec((None, 2, t // tk, tk), lambda p, i: (p, 0, 0, 0))],
        out_specs=pl.BlockSpec((tq, LANES), lambda p, i: (i, p)),
        out_shape=jax.ShapeDtypeStruct((t, FOX_WIDTH), BF16),
        scratch_shapes=[pltpu.VMEM((2 * tq, LANES), BF16), pltpu.VMEM((2, tq, LANES), F32),
                        pltpu.VMEM((2, tq, LANES), F32)],
        compiler_params=_params(("arbitrary", "arbitrary")),
        name="fox_prompt",
    )(q, kt, v, fk)


PAGES_PER_STEP = 8


def _suffix_kernel(lf_ref, m_ref, o_ref):
    l0, l1, l2 = _split3(lf_ref[...] * LOG2E)
    m = m_ref[...]
    o_ref[...] = _dot(l0, m) + (_dot(l1, m) + _dot(l2, m))


def _page_suffix(logf_t):
    n_phys = logf_t.shape[0]
    rows = n_phys * FOX_HEADS
    src = jnp.arange(PAGE_SIZE)
    mat = (src[:, None] >= src[None, :]).astype(BF16)
    br = 4096
    out = pl.pallas_call(
        _suffix_kernel,
        grid=(rows // br,),
        in_specs=[pl.BlockSpec((br, PAGE_SIZE), lambda i: (i, 0)),
                  pl.BlockSpec((PAGE_SIZE, PAGE_SIZE), lambda i: (0, 0))],
        out_specs=pl.BlockSpec((br, PAGE_SIZE), lambda i: (i, 0)),
        out_shape=jax.ShapeDtypeStruct((rows, PAGE_SIZE), F32),
        compiler_params=_params(("arbitrary",)),
        name="page_suffix",
    )(logf_t.reshape(rows, PAGE_SIZE), mat)
    return out.reshape(n_phys, FOX_HEADS, PAGE_SIZE)


SEQS_PER_STEP = 2


def _fox_sample_kernel(pt_ref, q_ref, kn_ref, vn_ref, fn_ref, suf_ref, *refs, n_pages):
    ns, gp = SEQS_PER_STEP, PAGES_PER_STEP
    k_refs = refs[0:ns * gp]
    v_refs = refs[ns * gp:2 * ns * gp]
    o_ref, qbd_ref, m_ref, l_ref, acc_ref, base_ref = refs[2 * ns * gp:]
    bi = pl.program_id(0)
    j = pl.program_id(1)
    nh, nq, hd = FOX_HEADS, q_ref.shape[1], FOX_HEAD_DIM
    rows = nh * nq
    seqs = range(ns)
    lane_w = lax.broadcasted_iota(jnp.int32, (rows, FOX_WIDTH), 1)
    row_w = lax.broadcasted_iota(jnp.int32, (rows, FOX_WIDTH), 0)
    own = (lane_w // hd) == (row_w // nq)

    @pl.when(j == 0)
    def _():
        for sq in seqs:
            qrep = jnp.broadcast_to(q_ref[sq][None], (nh, nq, FOX_WIDTH)).reshape(rows, FOX_WIDTH)
            qbd_ref[sq] = jnp.where(own, qrep, jnp.zeros_like(qrep))
        m_ref[...] = jnp.full(m_ref.shape, NEG_BIG, F32)
        l_ref[...] = jnp.zeros(l_ref.shape, F32)
        acc_ref[...] = jnp.zeros(acc_ref.shape, F32)
        base_ref[...] = jnp.zeros(base_ref.shape, F32)

    def update(scores, values, v_transposed):
        m_prev = [m_ref[sq] for sq in seqs]
        m_new = [jnp.maximum(m_prev[sq], jnp.max(scores[sq], axis=-1, keepdims=True)) for sq in seqs]
        p = [jnp.exp2(scores[sq] - m_new[sq]) for sq in seqs]
        pv = [(_dot_nt if v_transposed else _dot)(p[sq].astype(BF16), values[sq]) for sq in seqs]
        for sq in seqs:
            alpha = jnp.exp2(m_prev[sq] - m_new[sq])
            l_ref[sq] = alpha * l_ref[sq] + jnp.sum(p[sq], axis=-1, keepdims=True)
            acc_ref[sq] = alpha * acc_ref[sq] + pv[sq]
            m_ref[sq] = m_new[sq]

    lane_p = lax.broadcasted_iota(jnp.int32, (nh, PAGE_SIZE), 1)
    scores, values = [], []
    for sq in seqs:
        qbd = qbd_ref[sq]
        base = base_ref[sq]
        sc, va = [], []
        for g in range(gp):
            page = pt_ref[bi * ns + sq, n_pages - 1 - (j * gp + g)]
            incl = suf_ref[page]
            bias = jnp.where(lane_p < PAGE_SIZE - 1, pltpu.roll(incl, PAGE_SIZE - 1, axis=1), 0.0) + base
            base = base + incl[:, 0:1]
            kt = k_refs[sq * gp + g][...].reshape(FOX_WIDTH, PAGE_SIZE).astype(BF16)
            s = _dot(qbd, kt)
            sc.append((s.reshape(nh, nq, PAGE_SIZE) + bias[:, None, :]).reshape(rows, PAGE_SIZE))
            va.append(v_refs[sq * gp + g][...].reshape(FOX_WIDTH, PAGE_SIZE).astype(BF16))
        base_ref[sq] = base
        scores.append(jnp.concatenate(sc, axis=1))
        values.append(jnp.concatenate(va, axis=1))
    update(scores, values, True)

    @pl.when(j == pl.num_programs(1) - 1)
    def _():
        zpad = jnp.zeros((PAGE_SIZE - nq, FOX_WIDTH), BF16)
        qpos = lax.broadcasted_iota(jnp.int32, (nh, nq, PAGE_SIZE), 1)
        kpos = lax.broadcasted_iota(jnp.int32, (nh, nq, PAGE_SIZE), 2)
        sc, va = [], []
        for sq in seqs:
            kn = jnp.concatenate([kn_ref[sq], zpad], axis=0)
            s = _dot_nt(qbd_ref[sq], kn).reshape(nh, nq, PAGE_SIZE) - fn_ref[sq][:, None, :]
            sc.append(jnp.where(kpos <= qpos, s, NEG_BIG).reshape(rows, PAGE_SIZE))
            va.append(jnp.concatenate([vn_ref[sq], zpad], axis=0))
        update(sc, va, False)
        for sq in seqs:
            o = acc_ref[sq] / l_ref[sq]
            o = jnp.where(own, o, 0.0).reshape(nh, nq, FOX_WIDTH)
            o_ref[sq] = jnp.sum(o, axis=0).astype(o_ref.dtype)


def _fox_sample(page_table, q, k_new, v_new, f_new, cache_k, cache_v, suffix):
    b, t, w = q.shape
    n_pages = page_table.shape[1]
    ns, gp = SEQS_PER_STEP, PAGES_PER_STEP
    n_steps = n_pages // gp

    def page_map(sq, g):
        return lambda bi, j, pt: (pt[bi * ns + sq, n_pages - 1 - (j * gp + g)], 0, 0, 0)

    seq = lambda d1, d2: pl.BlockSpec((ns, d1, d2), lambda bi, j, pt: (bi, 0, 0))
    in_specs = [seq(t, w), seq(t, w), seq(t, w), seq(FOX_HEADS, PAGE_SIZE),
                pl.BlockSpec(suffix.shape, lambda bi, j, pt: (0, 0, 0), pipeline_mode=pl.Buffered(1))]
    kv_page = (None, FOX_HEADS, FOX_HEAD_DIM, PAGE_SIZE)
    pages = [(sq, g) for sq in range(ns) for g in range(gp)]
    in_specs += [pl.BlockSpec(kv_page, page_map(sq, g)) for sq, g in pages]
    in_specs += [pl.BlockSpec(kv_page, page_map(sq, g)) for sq, g in pages]
    rows = FOX_HEADS * t
    return pl.pallas_call(
        functools.partial(_fox_sample_kernel, n_pages=n_pages),
        grid_spec=pltpu.PrefetchScalarGridSpec(
            num_scalar_prefetch=1,
            grid=(b // ns, n_steps),
            in_specs=in_specs,
            out_specs=seq(t, w),
            scratch_shapes=[pltpu.VMEM((ns, rows, w), BF16), pltpu.VMEM((ns, rows, 1), F32),
                            pltpu.VMEM((ns, rows, 1), F32), pltpu.VMEM((ns, rows, w), F32),
                            pltpu.VMEM((ns, FOX_HEADS, PAGE_SIZE), F32)]),
        out_shape=jax.ShapeDtypeStruct((b, t, w), BF16),
        compiler_params=_params(("arbitrary", "arbitrary")),
        name="fox_sample",
    )(page_table, q, k_new, v_new, f_new, suffix, *([cache_k] * len(pages)), *([cache_v] * len(pages)))


GDN_CHUNK = 128
_HALO = 8


def _unit_lower_inverses(strict_lowers, row, col):
    c = strict_lowers[0].shape[0]
    eye = jnp.where(row == col, 1.0, 0.0)
    base = (row // 2) == (col // 2)
    xs = [eye - jnp.where(base, lm, 0.0) for lm in strict_lowers]
    s = 2
    while s < c:
        off_mask = ((row // (2 * s)) == (col // (2 * s))) & ((row // s) != (col // s))
        split = [_split2(x) for x in xs]
        offs = [jnp.where(off_mask, lm, 0.0).astype(BF16) for lm in strict_lowers]
        ts = [(_dot(xh, off) + _dot(xl, off)).astype(BF16) for (xh, xl), off in zip(split, offs)]
        xs = [x - (_dot(t, xh) + _dot(t, xl)) for x, t, (xh, xl) in zip(xs, ts, split)]
        s *= 2
    return xs


def _gdn_kernel(u_ref, small_ref, smallt_ref, z_ref, s0_ref, conv0_ref, convw_ref, normw_ref,
                o_ref, sout_ref, s_scr, ext_scr, *, per_seq):
    i = pl.program_id(0)
    c = GDN_CHUNK
    hd = GDN_HEAD_DIM
    heads = range(GDN_HEADS)

    def load_state():
        s_scr[...] = s0_ref[...]
        ext_scr[0:_HALO, :] = conv0_ref[...]

    if per_seq:
        load_state()
    else:
        pl.when(i == 0)(load_state)

    u = u_ref[...]
    ext_scr[_HALO:_HALO + c, :] = u
    y = u * convw_ref[GDN_CONV - 1:GDN_CONV, :]
    for tap in range(GDN_CONV - 1):
        back = GDN_CONV - 1 - tap
        y = y + ext_scr[_HALO - back:_HALO - back + c, :] * convw_ref[tap:tap + 1, :]
    ext_scr[0:_HALO, :] = u[c - _HALO:c, :]
    y = _silu(y)

    row = lax.broadcasted_iota(jnp.int32, (c, c), 0)
    col = lax.broadcasted_iota(jnp.int32, (c, c), 1)
    incl = row >= col
    small = small_ref[...]
    tril = jnp.where(incl, 1.0, 0.0)
    triu = jnp.where(row <= col, 1.0, 0.0)
    tril = tril.astype(BF16)
    triu = triu.astype(BF16)
    c0, c1, c2 = _split3(small)
    gc_cols = _dot(tril, c0) + (_dot(tril, c1) + _dot(tril, c2))
    r0, r1, r2 = _split3(smallt_ref[...])
    gc_rows = _dot(r0, triu) + (_dot(r1, triu) + _dot(r2, triu))

    def head_cols(base, h):
        return y[:, base + h * hd:base + (h + 1) * hd]

    def unit(x):
        return x * lax.rsqrt(jnp.sum(x * x, axis=-1, keepdims=True) + NORM_EPS)

    q = [unit(head_cols(0, h)) * hd ** -0.5 for h in heads]
    k = [unit(head_cols(GDN_WIDTH, h)) for h in heads]
    v = [head_cols(2 * GDN_WIDTH, h) for h in heads]
    beta = [small[:, 12 + h:13 + h] for h in heads]
    gc = [gc_cols[:, 8 + h:9 + h] for h in heads]
    g_last = [g[c - 1:c, :] for g in gc]
    decay = [jnp.where(incl, jnp.exp(jnp.where(incl, gc[h] - gc_rows[8 + h:9 + h, :], 0.0)), 0.0) for h in heads]
    e_gc = [jnp.exp(g) for g in gc]
    kb = [k[h] * beta[h] for h in heads]
    k_b = [x.astype(BF16) for x in k]
    kb_s = [_split2(x) for x in kb]
    lmat = [jnp.where(row > col, (_dot_nt(kb_s[h][0], k_b[h]) + _dot_nt(kb_s[h][1], k_b[h])) * decay[h], 0.0)
            for h in heads]
    a_intra = [jnp.where(incl, _dot_nt(q[h].astype(BF16), k_b[h]) * decay[h], 0.0) for h in heads]
    tinv = [_split2(x) for x in _unit_lower_inverses(lmat, row, col)]
    vb = [(v[h] * beta[h]).astype(BF16) for h in heads]
    kg = [(kb[h] * e_gc[h]).astype(BF16) for h in heads]
    uu = [_dot(tinv[h][0], vb[h]) + _dot(tinv[h][1], vb[h]) for h in heads]
    ww = [(_dot(tinv[h][0], kg[h]) + _dot(tinv[h][1], kg[h])).astype(BF16) for h in heads]
    state = [s_scr[h] for h in heads]
    st_s = [_split2(x) for x in state]
    v_new = [uu[h] - (_dot(ww[h], st_s[h][0]) + _dot(ww[h], st_s[h][1])) for h in heads]
    o = [_dot((q[h] * e_gc[h]).astype(BF16), st_s[h][0]) + _dot(a_intra[h].astype(BF16), v_new[h].astype(BF16))
         for h in heads]
    for h in heads:
        kd = (k[h] * jnp.exp(g_last[h] - gc[h])).astype(BF16)
        vh, vl = _split2(v_new[h])
        s_scr[h] = state[h] * jnp.exp(g_last[h]) + (_dot_tn(kd, vh) + _dot_tn(kd, vl))
    for h in heads:
        sl = slice(h * hd, (h + 1) * hd)
        on = o[h] * lax.rsqrt(jnp.mean(o[h] * o[h], axis=-1, keepdims=True) + NORM_EPS) * normw_ref[...]
        o_ref[:, sl] = (on * _silu(z_ref[:, sl])).astype(o_ref.dtype)

    sout_ref[...] = s_scr[...]


def _gdn(u, small, small_t, z, s0, conv0, conv_w, norm_w, *, per_seq):
    c = GDN_CHUNK
    n = u.shape[0] // c
    nb = s0.shape[0]
    bsel = (lambda i: i) if per_seq else (lambda i: 0)
    return pl.pallas_call(
        functools.partial(_gdn_kernel, per_seq=per_seq),
        grid=(n,),
        in_specs=[pl.BlockSpec((c, 3 * GDN_WIDTH), lambda i: (i, 0)),
                  pl.BlockSpec((c, LANES), lambda i: (i, 0)),
                  pl.BlockSpec((16, c), lambda i: (0, i)),
                  pl.BlockSpec((c, GDN_WIDTH), lambda i: (i, 0)),
                  pl.BlockSpec((None, GDN_HEADS, GDN_HEAD_DIM, GDN_HEAD_DIM), lambda i: (bsel(i), 0, 0, 0)),
                  pl.BlockSpec((None, _HALO, 3 * GDN_WIDTH), lambda i: (bsel(i), 0, 0)),
                  pl.BlockSpec((8, 3 * GDN_WIDTH), lambda i: (0, 0)),
                  pl.BlockSpec((1, GDN_HEAD_DIM), lambda i: (0, 0))],
        out_specs=[pl.BlockSpec((c, GDN_WIDTH), lambda i: (i, 0)),
                   pl.BlockSpec((None, GDN_HEADS, GDN_HEAD_DIM, GDN_HEAD_DIM), lambda i: (bsel(i), 0, 0, 0))],
        out_shape=[jax.ShapeDtypeStruct((n * c, GDN_WIDTH), BF16),
                   jax.ShapeDtypeStruct((nb, GDN_HEADS, GDN_HEAD_DIM, GDN_HEAD_DIM), F32)],
        scratch_shapes=[pltpu.VMEM((GDN_HEADS, GDN_HEAD_DIM, GDN_HEAD_DIM), F32),
                        pltpu.VMEM((_HALO + c, 3 * GDN_WIDTH), F32)],
        compiler_params=_params(("arbitrary",)),
        name="gdn",
    )(u, small, small_t, z, s0, conv0, conv_w, norm_w)


_GROUP_LANE0 = N_EXPERTS


def _layer_norm(y, g, b):
    mu = jnp.mean(y, axis=-1, keepdims=True)
    d = y - mu
    var = jnp.mean(d * d, axis=-1, keepdims=True)
    return d * lax.rsqrt(var + LN_EPS) * g + b


def _pack_router(w_grp, b_grp, w_rt, b_rt):
    d = w_grp.shape[0]
    w = jnp.concatenate([w_rt, w_grp, jnp.zeros((d, LANES - N_EXPERTS - N_GROUPS), F32)], axis=1)
    b = jnp.concatenate([b_rt, b_grp, jnp.zeros((LANES - N_EXPERTS - N_GROUPS,), F32)]).reshape(1, LANES)
    wh = w.astype(BF16)
    wl = (w - wh.astype(F32)).astype(BF16)
    return wh, wl, b


def _mix_kernel(x_ref, of_ref, og_ref, gf_ref, gg_ref, wuf_ref, wug_ref, wo_ref, gate_ref, lng_ref, lnb_ref,
                shift_ref, scale_ref, wrh_ref, wrl_ref, br_ref,
                x1_ref, h2_ref, route_ref, sel_ref):
    merged = (gf_ref[...].astype(F32) * _dot(of_ref[...], wuf_ref[...])
              + gg_ref[...].astype(F32) * _dot(og_ref[...], wug_ref[...]))
    mix = _dot(merged.astype(BF16), wo_ref[...])
    x1 = _layer_norm(DEEPNORM_ALPHA * x_ref[...] + (1.0 + gate_ref[...]) * mix, lng_ref[...], lnb_ref[...])
    x1_ref[...] = x1
    h2 = x1 * (1.0 + scale_ref[...]) + shift_ref[...]
    h2_ref[...] = h2

    hh, hl = _split2(h2)
    wrh = wrh_ref[...]
    logits = _dot(hh, wrh) + (_dot(hh, wrl_ref[...]) + _dot(hl, wrh)) + br_ref[...]
    lane = lax.broadcasted_iota(jnp.int32, logits.shape, 1)
    big = jnp.int32(LANES)
    neg = jnp.float32(-jnp.inf)

    def top1(vals):
        m = jnp.max(vals, axis=-1, keepdims=True)
        idx = jnp.min(jnp.where(vals == m, lane, big), axis=-1, keepdims=True)
        return m, idx

    is_grp = (lane >= _GROUP_LANE0) & (lane < _GROUP_LANE0 + N_GROUPS)
    glog = jnp.where(is_grp, logits, neg)
    gmax, gidx = top1(glog)
    grp_p = 1.0 / jnp.sum(jnp.exp(glog - gmax), axis=-1, keepdims=True)
    grp = gidx - _GROUP_LANE0
    elog = jnp.where((lane // EXPERTS_PER_GROUP) == grp, logits, neg)
    m1, i1 = top1(elog)
    m2, i2 = top1(jnp.where(lane == i1, neg, elog))
    e21 = jnp.exp(m2 - m1)
    w1 = grp_p / (1.0 + e21)
    w2 = grp_p * e21 / (1.0 + e21)
    route = jnp.where(lane == 0, i1.astype(F32),
                      jnp.where(lane == 1, i2.astype(F32),
                                jnp.where(lane == 2, w1, jnp.where(lane == 3, w2, 0.0))))
    route_ref[...] = route
    sel_ref[...] = jnp.where((lane == i1) | (lane == i2), 1.0, 0.0).astype(BF16)


def _mix(x, o_fox, o_gdn, gf, gg, wuf, wug, wo, gate, ln_g, ln_b, shift, scale, wrh, wrl, br, *, tm):
    t, d = x.shape
    per_tok = gate.shape[0] != 1
    mod_spec = (pl.BlockSpec((tm, d), lambda i: (i, 0)) if per_tok
                else pl.BlockSpec((1, d), lambda i: (0, 0)))
    const = lambda a: pl.BlockSpec(a.shape, lambda i: (0, 0))
    tile = lambda n: pl.BlockSpec((tm, n), lambda i: (i, 0))
    return pl.pallas_call(
        _mix_kernel,
        grid=(t // tm,),
        in_specs=[tile(d), tile(FOX_WIDTH), tile(GDN_WIDTH), tile(d), tile(d), const(wuf), const(wug), const(wo),
                  mod_spec, const(ln_g), const(ln_b), mod_spec, mod_spec, const(wrh), const(wrl), const(br)],
        out_specs=[tile(d), tile(d), tile(LANES), tile(LANES)],
        out_shape=[jax.ShapeDtypeStruct((t, d), F32), jax.ShapeDtypeStruct((t, d), F32),
                   jax.ShapeDtypeStruct((t, LANES), F32), jax.ShapeDtypeStruct((t, LANES), BF16)],
        compiler_params=_params(("arbitrary",)),
        name="mix_ln_route",
    )(x, o_fox, o_gdn, gf, gg, wuf, wug, wo, gate, ln_g, ln_b, shift, scale, wrh, wrl, br)


MOE_ROWS = 256


def _rank_kernel(sel_ref, route_ref, out_ref, cnt_ref, carry_ref):
    i = pl.program_id(0)
    tm = sel_ref.shape[0]

    @pl.when(i == 0)
    def _():
        carry_ref[...] = jnp.zeros_like(carry_ref)

    sel = sel_ref[...]
    row = lax.broadcasted_iota(jnp.int32, (tm, tm), 0)
    col = lax.broadcasted_iota(jnp.int32, (tm, tm), 1)
    strict = jnp.where(row > col, 1.0, 0.0).astype(BF16)
    rank = _dot(strict, sel) + carry_ref[0:1, :]
    lane = lax.broadcasted_iota(jnp.int32, rank.shape, 1)
    route = route_ref[...]
    e0 = route[:, 0:1].astype(jnp.int32)
    e1 = route[:, 1:2].astype(jnp.int32)
    r0 = jnp.sum(jnp.where(lane == e0, rank, 0.0), axis=-1, keepdims=True)
    r1 = jnp.sum(jnp.where(lane == e1, rank, 0.0), axis=-1, keepdims=True)
    out_ref[...] = jnp.where(lane == 0, r0, jnp.where(lane == 1, r1, 0.0))
    total = rank[tm - 1:tm, :] + sel[tm - 1:tm, :].astype(F32)
    carry_ref[0:1, :] = total
    cnt_ref[...] = jnp.broadcast_to(total, cnt_ref.shape)


def _rank(sel, route, *, tm):
    t = sel.shape[0]
    tile = pl.BlockSpec((tm, LANES), lambda i: (i, 0))
    return pl.pallas_call(
        _rank_kernel,
        grid=(t // tm,),
        in_specs=[tile, tile],
        out_specs=[tile, pl.BlockSpec((8, LANES), lambda i: (0, 0))],
        out_shape=[jax.ShapeDtypeStruct((t, LANES), F32), jax.ShapeDtypeStruct((8, LANES), F32)],
        scratch_shapes=[pltpu.VMEM((8, LANES), F32)],
        compiler_params=_params(("arbitrary",)),
        name="moe_rank",
    )(sel, route)


def _scatter_kernel(pos_ref, h_ref, init_ref, xs_ref, sem, *, tok0, n_tok):
    del init_ref
    i = pl.program_id(0)
    tm = h_ref.shape[0]

    def copy(r, slot):
        dst = pos_ref[slot * n_tok + tok0 + i * tm + r]
        return pltpu.make_async_copy(h_ref.at[pl.ds(r, 1), :], xs_ref.at[pl.ds(dst, 1), :], sem)

    def start(r, c):
        copy(r, 0).start()
        copy(r, 1).start()
        return c

    def wait(r, c):
        copy(r, 0).wait()
        copy(r, 1).wait()
        return c

    lax.fori_loop(0, tm, start, 0, unroll=8)
    lax.fori_loop(0, tm, wait, 0, unroll=8)


def _scatter_rows(pos, h, init, *, tm, tok0):
    t, d = h.shape
    n_sorted = init.shape[0]
    return pl.pallas_call(
        functools.partial(_scatter_kernel, tok0=tok0, n_tok=pos.shape[0] // 2),
        grid_spec=pltpu.PrefetchScalarGridSpec(
            num_scalar_prefetch=1,
            grid=(t // tm,),
            in_specs=[pl.BlockSpec((tm, d), lambda i, pos: (i, 0)),
                      pl.BlockSpec(memory_space=pl.ANY)],
            out_specs=pl.BlockSpec(memory_space=pl.ANY),
            scratch_shapes=[pltpu.SemaphoreType.DMA(())]),
        out_shape=jax.ShapeDtypeStruct((n_sorted, d), F32),
        input_output_aliases={2: 0},
        compiler_params=_params(("arbitrary",)),
        name="moe_scatter",
    )(pos, h, init)


def _experts_kernel(te_ref, x_ref, wg_ref, wu_ref, wd_ref, y_ref):
    i = pl.program_id(0)

    @pl.when(te_ref[1, i] > 0)
    def _():
        xb = x_ref[...].astype(BF16)
        act = _silu(_dot(xb, wg_ref[...].astype(BF16))) * _dot(xb, wu_ref[...].astype(BF16))
        y_ref[...] = _dot(act.astype(BF16), wd_ref[...].astype(BF16))

    @pl.when(te_ref[1, i] == 0)
    def _():
        y_ref[...] = jnp.zeros_like(y_ref)


def _experts(tile_info, xs, w_g, w_u, w_d):
    n_sorted, d = xs.shape
    ff = w_g.shape[2]
    n_tiles = n_sorted // MOE_ROWS
    return pl.pallas_call(
        _experts_kernel,
        grid_spec=pltpu.PrefetchScalarGridSpec(
            num_scalar_prefetch=1,
            grid=(n_tiles,),
            in_specs=[pl.BlockSpec((MOE_ROWS, d), lambda i, te: (i, 0)),
                      pl.BlockSpec((None, d, ff), lambda i, te: (te[0, i], 0, 0)),
                      pl.BlockSpec((None, d, ff), lambda i, te: (te[0, i], 0, 0)),
                      pl.BlockSpec((None, ff, d), lambda i, te: (te[0, i], 0, 0))],
            out_specs=pl.BlockSpec((MOE_ROWS, d), lambda i, te: (i, 0))),
        out_shape=jax.ShapeDtypeStruct((n_sorted, d), F32),
        compiler_params=_params(("arbitrary",)),
        name="moe_experts",
    )(tile_info, xs, w_g, w_u, w_d)


def _combine_kernel(pos_ref, ys_ref, x1_ref, route_ref, gate_ref, lng_ref, lnb_ref, out_ref, buf, sem,
                    *, tok0, n_tok):
    i = pl.program_id(0)
    tm = x1_ref.shape[0]

    def copy(r, slot):
        src = pos_ref[slot * n_tok + tok0 + i * tm + r]
        return pltpu.make_async_copy(ys_ref.at[pl.ds(src, 1), :], buf.at[slot, pl.ds(r, 1), :], sem)

    def start(r, c):
        copy(r, 0).start()
        copy(r, 1).start()
        return c

    def wait(r, c):
        copy(r, 0).wait()
        copy(r, 1).wait()
        return c

    lax.fori_loop(0, tm, start, 0, unroll=8)
    lax.fori_loop(0, tm, wait, 0, unroll=8)
    route = route_ref[...]
    ffn = route[:, 2:3] * buf[0] + route[:, 3:4] * buf[1]
    y = DEEPNORM_ALPHA * x1_ref[...] + (1.0 + gate_ref[...]) * ffn
    out_ref[...] = _layer_norm(y, lng_ref[...], lnb_ref[...])


def _combine(pos, ys, x1, route, gate, ln_g, ln_b, *, tm, tok0):
    t, d = x1.shape
    gate_spec = (pl.BlockSpec((tm, d), lambda i, pos: (i, 0)) if gate.shape[0] != 1
                 else pl.BlockSpec((1, d), lambda i, pos: (0, 0)))
    return pl.pallas_call(
        functools.partial(_combine_kernel, tok0=tok0, n_tok=pos.shape[0] // 2),
        grid_spec=pltpu.PrefetchScalarGridSpec(
            num_scalar_prefetch=1,
            grid=(t // tm,),
            in_specs=[pl.BlockSpec(memory_space=pl.ANY),
                      pl.BlockSpec((tm, d), lambda i, pos: (i, 0)),
                      pl.BlockSpec((tm, LANES), lambda i, pos: (i, 0)),
                      gate_spec,
                      pl.BlockSpec((1, d), lambda i, pos: (0, 0)),
                      pl.BlockSpec((1, d), lambda i, pos: (0, 0))],
            out_specs=pl.BlockSpec((tm, d), lambda i, pos: (i, 0)),
            scratch_shapes=[pltpu.VMEM((2, tm, d), F32), pltpu.SemaphoreType.DMA(())]),
        out_shape=jax.ShapeDtypeStruct((t, d), F32),
        compiler_params=_params(("arbitrary",)),
        name="moe_combine_ln",
    )(pos, ys, x1, route, gate, ln_g, ln_b)


def _moe(groups, w_g, w_u, w_d, ln_g, ln_b, *, tm):
    route = jnp.concatenate([g[2] for g in groups], axis=0)
    sel = jnp.concatenate([g[3] for g in groups], axis=0)
    n = route.shape[0]
    ranks, counts = _rank(sel, route, tm=tm)
    counts = counts[0, :N_EXPERTS].astype(jnp.int32)
    padded = ((counts + MOE_ROWS - 1) // MOE_ROWS) * MOE_ROWS
    ends = jnp.cumsum(padded)
    offsets = ends - padded
    n_tiles = (2 * n) // MOE_ROWS + N_EXPERTS
    tile_start = jnp.arange(n_tiles, dtype=jnp.int32) * MOE_ROWS
    tile_expert = jnp.sum((tile_start[:, None] >= ends[None, :]).astype(jnp.int32), axis=1)
    tile_expert = jnp.minimum(tile_expert, N_EXPERTS - 1)
    tile_used = (tile_start < ends[-1]).astype(jnp.int32)
    tile_info = jnp.stack([tile_expert, tile_used])
    e = route[:, 0:2].astype(jnp.int32)
    r = ranks[:, 0:2].astype(jnp.int32)
    pos = (offsets[e] + r).T.reshape(-1)
    xs = jnp.zeros((n_tiles * MOE_ROWS, h2_dim(groups)), F32)
    tok0 = 0
    for h2, _, _, _, _ in groups:
        xs = _scatter_rows(pos, h2, xs, tm=tm, tok0=tok0)
        tok0 += h2.shape[0]
    ys = _experts(tile_info, xs, w_g, w_u, w_d)
    outs = []
    tok0 = 0
    for _, x1, rt, _, gate in groups:
        outs.append(_combine(pos, ys, x1, rt, gate, ln_g, ln_b, tm=tm, tok0=tok0))
        tok0 += x1.shape[0]
    return outs


def h2_dim(groups):
    return groups[0][0].shape[1]


TOKEN_TILE = 256
ATTN_TILE = 512


def kernel(x_prompt, x_sample, cache_fox_k, cache_fox_v, cache_fox_logf, state_gdn, state_gdn_conv, page_table,
           c_prompt, c_sample, w_ada_mix, b_ada_mix, w_in, b_forget, gdn_conv_w, gdn_a_log, gdn_dt_bias,
           gdn_norm_w, w_up_fox, w_up_gdn, w_out, ln1_g, ln1_b, w_ada_ffn, b_ada_ffn, w_group_router,
           b_group_router, w_expert_router, b_expert_router, w_expert_gate, w_expert_up, w_expert_down,
           ln2_g, ln2_b):
    assert w_in.shape[0] == 1, "single-layer step"
    bp, tp, d = x_prompt.shape
    bs, ts, _ = x_sample.shape
    assert bp == 1
    tm = TOKEN_TILE
    row = lambda a: a.reshape(1, -1)

    n_c = bp + bs
    c_all = jnp.concatenate([c_prompt, c_sample, jnp.zeros((-n_c % 8, d), F32)], axis=0)
    mod_mix = _adaln(c_all, w_ada_mix[0], b_ada_mix[0])
    mod_ffn = _adaln(c_all, w_ada_ffn[0], b_ada_ffn[0])

    def split_mod(mod, lo, hi, rep):
        m = mod[lo:hi]
        if rep > 1:
            m = jnp.repeat(m, rep, axis=0)
        return m[:, 0:d], m[:, d:2 * d], m[:, 2 * d:3 * d]

    wide, wsh, wsl = _pack_w_in(jnp.transpose(w_in[0]))
    prow, pcol = _param_rows(b_forget[0], gdn_a_log[0], gdn_dt_bias[0])
    conv_w = jnp.pad(gdn_conv_w[0], ((0, 8 - GDN_CONV), (0, 0)))
    norm_w = row(gdn_norm_w[0])
    wuf, wug, wo = w_up_fox[0].astype(BF16), w_up_gdn[0].astype(BF16), w_out[0].astype(BF16)
    wrh, wrl, br = _pack_router(w_group_router[0], b_group_router[0], w_expert_router[0], b_expert_router[0])
    ln1 = (row(ln1_g[0]), row(ln1_b[0]))
    heads = (FOX_HEADS, FOX_HEAD_DIM)

    shift1, scale1, gate1 = split_mod(mod_mix, 0, 1, 1)
    shift2, scale2, gate2_p = split_mod(mod_ffn, 0, 1, 1)
    xp = x_prompt[0]
    pr = _project(xp, shift1, scale1, wide, wsh, wsl, prow, pcol, tm=ATTN_TILE, seg_len=None, kv_transposed=True)
    o_fox = _fox_prompt(pr['q'], pr['ktb'], pr['vb'], pr['ft'][0:FOX_HEADS], tq=ATTN_TILE)
    o_gdn, state_p = _gdn(pr['gqkv'], pr['small'], pr['small_t'], pr['gz'],
                          jnp.zeros((1, GDN_HEADS, GDN_HEAD_DIM, GDN_HEAD_DIM), F32),
                          jnp.zeros((1, _HALO, 3 * GDN_WIDTH), F32), conv_w, norm_w, per_seq=False)
    x1_p, h2_p, route_p, sel_p = _mix(xp, o_fox, o_gdn, pr['gf'], pr['gg'], wuf, wug, wo, gate1, *ln1,
                                      shift2, scale2, wrh, wrl, br, tm=tm)
    head_major = lambda a: jnp.transpose(a.reshape(*heads, tp), (2, 0, 1))
    prompt_rows = (head_major(pr['ktf']), head_major(pr['vtf']), jnp.transpose(pr['small_t'][0:FOX_HEADS]),
                   state_p, pr['gqkv'][tp - (GDN_CONV - 1):])

    n_s = bs * ts
    shift1, scale1, gate1 = split_mod(mod_mix, 1, 1 + bs, ts)
    shift2, scale2, gate2_s = split_mod(mod_ffn, 1, 1 + bs, ts)
    xs = x_sample.reshape(n_s, d)
    sr = _project(xs, shift1, scale1, wide, wsh, wsl, prow, pcol, tm=n_s, seg_len=ts, kv_transposed=False)
    suffix = _page_suffix(jnp.transpose(cache_fox_logf[0], (0, 2, 1)))
    lanes_per_seq = lambda a: a.reshape(a.shape[0], bs, ts)
    f_new = jnp.pad(jnp.transpose(lanes_per_seq(sr['ft'][0:FOX_HEADS]), (1, 0, 2)),
                    ((0, 0), (0, 0), (0, PAGE_SIZE - ts)))
    seq3 = lambda a: a.reshape(bs, ts, a.shape[-1])
    o_fox = _fox_sample(page_table, seq3(sr['q']), seq3(sr['kb']), seq3(sr['vb']), f_new,
                        jnp.transpose(cache_fox_k[0], (0, 2, 3, 1)), jnp.transpose(cache_fox_v[0], (0, 2, 3, 1)),
                        suffix).reshape(n_s, FOX_WIDTH)
    chunk = lambda a: jnp.pad(seq3(a), ((0, 0), (0, GDN_CHUNK - ts), (0, 0))).reshape(bs * GDN_CHUNK, a.shape[-1])
    small_t_c = jnp.pad(lanes_per_seq(sr['small_t']), ((0, 0), (0, 0), (0, GDN_CHUNK - ts))).reshape(N_SMALL, -1)
    conv0 = jnp.pad(state_gdn_conv[0], ((0, 0), (_HALO - (GDN_CONV - 1), 0), (0, 0)))
    o_gdn, state_s = _gdn(chunk(sr['gqkv']), chunk(sr['small']), small_t_c, chunk(sr['gz']), state_gdn[0], conv0,
                          conv_w, norm_w, per_seq=True)
    o_gdn = o_gdn.reshape(bs, GDN_CHUNK, GDN_WIDTH)[:, 0:ts].reshape(n_s, GDN_WIDTH)
    x1_s, h2_s, route_s, sel_s = _mix(xs, o_fox, o_gdn, sr['gf'], sr['gg'], wuf, wug, wo, gate1, *ln1,
                                      shift2, scale2, wrh, wrl, br, tm=n_s)
    sample_rows = (sr['kf'], sr['vf'], sr['small'][:, 0:FOX_HEADS], state_s,
                   seq3(sr['gqkv'])[:, ts - (GDN_CONV - 1):])

    y_p, y_s = _moe([(h2_p, x1_p, route_p, sel_p, gate2_p), (h2_s, x1_s, route_s, sel_s, gate2_s)],
                    w_expert_gate[0], w_expert_up[0], w_expert_down[0], row(ln2_g[0]), row(ln2_b[0]), tm=tm)

    kp, vp, lfp, sp, cvp = prompt_rows
    ks, vs, lfs, ss, cvs = sample_rows
    return (y_p.reshape(1, tp, d), y_s.reshape(bs, ts, d),
            kp.reshape(1, 1, tp, *heads), vp.reshape(1, 1, tp, *heads), lfp.reshape(1, 1, tp, FOX_HEADS),
            sp.reshape(1, 1, GDN_HEADS, GDN_HEAD_DIM, GDN_HEAD_DIM), cvp.reshape(1, 1, GDN_CONV - 1, 3 * GDN_WIDTH),
            ks.reshape(1, bs, ts, *heads), vs.reshape(1, bs, ts, *heads), lfs.reshape(1, bs, ts, FOX_HEADS),
            ss.reshape(1, bs, GDN_HEADS, GDN_HEAD_DIM, GDN_HEAD_DIM),
            cvs.reshape(1, bs, GDN_CONV - 1, 3 * GDN_WIDTH))
```

```python
import functools

import jax
import jax.numpy as jnp
from jax import lax
from jax.experimental import pallas as pl
from jax.experimental.pallas import tpu as pltpu

F32 = jnp.float32
BF16 = jnp.bfloat16

D_MODEL = 1024
FOX_HEADS = 8
FOX_HEAD_DIM = 64
FOX_WIDTH = FOX_HEADS * FOX_HEAD_DIM
GDN_HEADS = 4
GDN_HEAD_DIM = 128
GDN_WIDTH = GDN_HEADS * GDN_HEAD_DIM
GDN_CONV = 4
N_GROUPS = 4
EXPERTS_PER_GROUP = 8
N_EXPERTS = N_GROUPS * EXPERTS_PER_GROUP
EXPERT_FF = 512
PAGE_SIZE = 128
DEEPNORM_ALPHA = 2.0 ** 0.25
LN_EPS = 1e-5
NORM_EPS = 1e-6
LOG2E = 1.4426950408889634
LANES = 128
VMEM_LIMIT = 56 * 1024 * 1024

_C_FQ, _C_FK, _C_FV, _C_GQKV, _C_GZ, _C_GF, _C_GG, _C_END = 0, 512, 1024, 1536, 3072, 3584, 4608, 5632


def _params(sem):
    return pltpu.CompilerParams(dimension_semantics=sem, vmem_limit_bytes=VMEM_LIMIT)


def _dot(a, b):
    return jnp.dot(a, b, preferred_element_type=F32)


def _dot_nt(a, b):
    return lax.dot_general(a, b, (((1,), (1,)), ((), ())), preferred_element_type=F32)


def _dot_tn(a, b):
    return lax.dot_general(a, b, (((0,), (0,)), ((), ())), preferred_element_type=F32)


def _split2(x):
    hi = x.astype(BF16)
    lo = (x - hi.astype(F32)).astype(BF16)
    return hi, lo


def _split3(x):
    a = x.astype(BF16)
    r = x - a.astype(F32)
    b = r.astype(BF16)
    c = (r - b.astype(F32)).astype(BF16)
    return a, b, c


def _dot3(a, b):
    ah, al = _split2(a)
    bh, bl = _split2(b)
    return _dot(ah, bh) + (_dot(ah, bl) + _dot(al, bh))


def _dot3_nt(a, b):
    ah, al = _split2(a)
    bh, bl = _split2(b)
    return _dot_nt(ah, bh) + (_dot_nt(ah, bl) + _dot_nt(al, bh))


def _silu(x):
    return x * jax.nn.sigmoid(x)


def _softplus(x):
    return jnp.maximum(x, 0.0) + jnp.log1p(jnp.exp(-jnp.abs(x)))


def _log_sigmoid(x):
    return jnp.minimum(x, 0.0) - jnp.log1p(jnp.exp(-jnp.abs(x)))


def _ada_kernel(c_ref, w_ref, b_ref, o_ref):
    s = _silu(c_ref[...])
    o_ref[...] = jnp.dot(s, w_ref[...], preferred_element_type=F32,
                         precision=lax.Precision.HIGHEST) + b_ref[...]


def _adaln(c, w, b):
    rows, d = c.shape
    n = w.shape[1]
    bn = 512
    return pl.pallas_call(
        _ada_kernel,
        grid=(n // bn,),
        in_specs=[pl.BlockSpec((rows, d), lambda j: (0, 0)),
                  pl.BlockSpec((d, bn), lambda j: (0, j)),
                  pl.BlockSpec((1, bn), lambda j: (0, j))],
        out_specs=pl.BlockSpec((rows, bn), lambda j: (0, j)),
        out_shape=jax.ShapeDtypeStruct((rows, n), F32),
        compiler_params=_params(("arbitrary",)),
        name="adaln",
    )(c, w, b.reshape(1, n))


KEY_TILE = 256
N_SMALL = 16


def _pack_w_in(w_in_t):
    wide = jnp.concatenate([w_in_t[0:1536], w_in_t[1544:3592], w_in_t[3600:5648]], axis=0).astype(BF16)
    small = jnp.concatenate([w_in_t[1536:1544], w_in_t[3592:3600],
                             jnp.zeros((LANES - N_SMALL, w_in_t.shape[1]), F32)], axis=0)
    sh = small.astype(BF16)
    sl = (small - sh.astype(F32)).astype(BF16)
    return wide, sh, sl


def _param_rows(b_forget, a_log, dt_bias):
    z = jnp.zeros((LANES,), F32)
    bias = z.at[0:8].set(b_forget).at[8:12].set(dt_bias)
    nega = z.at[8:12].set(-jnp.exp(a_log))
    rows = jnp.zeros((8, LANES), F32).at[0].set(bias).at[1].set(nega)
    cols = jnp.zeros((N_SMALL, LANES), F32).at[:, 0].set(bias[0:N_SMALL]).at[:, 1].set(nega[0:N_SMALL])
    return rows, cols


def _small_act(z, idx, nega):
    return jnp.where(idx < 8, _log_sigmoid(z), jnp.where(idx < 12, nega * _softplus(z), jax.nn.sigmoid(z)))


def _proj_kernel(x_ref, shift_ref, scale_ref, w_ref, wsh_ref, wsl_ref, prow_ref, pcol_ref, *refs,
                 seg_len, kv_transposed):
    if kv_transposed:
        (q_ref, vb_ref, gqkv_ref, gz_ref, gf_ref, gg_ref, small_ref, smallt_ref, ft_ref,
         ktf_ref, ktb_ref, vtf_ref, carry_ref) = refs
    else:
        (q_ref, vb_ref, gqkv_ref, gz_ref, gf_ref, gg_ref, small_ref, smallt_ref, ft_ref,
         kf_ref, kb_ref, vf_ref, carry_ref) = refs
    i = pl.program_id(0)
    tm = x_ref.shape[0]

    @pl.when(i == 0)
    def _():
        carry_ref[...] = jnp.zeros_like(carry_ref)

    h = x_ref[...] * (1.0 + scale_ref[...]) + shift_ref[...]
    hb, hl = _split2(h)
    seg = lambda lo, hi: w_ref[lo:hi, :]

    q_ref[...] = (_dot_nt(hb, seg(_C_FQ, _C_FK)) * (LOG2E * FOX_HEAD_DIM ** -0.5)).astype(BF16)
    v = _dot_nt(hb, seg(_C_FV, _C_GQKV))
    vb_ref[...] = v.astype(BF16)
    if kv_transposed:
        kt = _dot_nt(seg(_C_FK, _C_FV), hb)
        ktf_ref[...] = kt
        ktbf = kt.astype(BF16)
        for c in range(ktb_ref.shape[0]):
            ktb_ref[c] = ktbf[:, c * KEY_TILE:(c + 1) * KEY_TILE]
        vtf_ref[...] = v.T
    else:
        k = _dot_nt(hb, seg(_C_FK, _C_FV))
        kf_ref[...] = k
        kb_ref[...] = k.astype(BF16)
        vf_ref[...] = v
    gqkv_ref[...] = _dot_nt(hb, seg(_C_GQKV, _C_GZ))
    gz_ref[...] = _dot_nt(hb, seg(_C_GZ, _C_GF))
    gf_ref[...] = jax.nn.sigmoid(_dot_nt(hb, seg(_C_GF, _C_GG))).astype(BF16)
    gg_ref[...] = jax.nn.sigmoid(_dot_nt(hb, seg(_C_GG, _C_END))).astype(BF16)

    wsh = wsh_ref[...]
    z = _dot_nt(hb, wsh) + (_dot_nt(hb, wsl_ref[...]) + _dot_nt(hl, wsh)) + prow_ref[0:1, :]
    lane = lax.broadcasted_iota(jnp.int32, z.shape, 1)
    small_ref[...] = _small_act(z, lane, prow_ref[1:2, :])

    ws16h = wsh_ref[0:N_SMALL, :]
    zt = (_dot_nt(ws16h, hb) + (_dot_nt(wsl_ref[0:N_SMALL, :], hb) + _dot_nt(ws16h, hl))) + pcol_ref[:, 0:1]
    rid = lax.broadcasted_iota(jnp.int32, zt.shape, 0)
    small_t = _small_act(zt, rid, pcol_ref[:, 1:2])
    smallt_ref[...] = small_t
    row = lax.broadcasted_iota(jnp.int32, (tm, tm), 0)
    col = lax.broadcasted_iota(jnp.int32, (tm, tm), 1)
    keep = row <= col
    if seg_len is not None:
        keep = keep & ((row // seg_len) == (col // seg_len))
    tri = jnp.where(keep, 1.0, 0.0).astype(BF16)
    l0, l1, l2 = _split3(jnp.where(rid < FOX_HEADS, small_t, 0.0))
    fcum = _dot(l0, tri) + (_dot(l1, tri) + _dot(l2, tri))
    if seg_len is None:
        fcum = fcum + carry_ref[:, 0:1]
        carry_ref[...] = jnp.broadcast_to(fcum[:, tm - 1:tm], carry_ref.shape)
    ft_ref[...] = fcum * LOG2E


def _project(x, shift, scale, wide, wsh, wsl, prow, pcol, *, tm, seg_len, kv_transposed):
    t, d = x.shape
    nt = t // tm
    per_tok = shift.shape[0] != 1
    mod_spec = (pl.BlockSpec((tm, d), lambda i: (i, 0)) if per_tok
                else pl.BlockSpec((1, d), lambda i: (0, 0)))
    const = lambda a: pl.BlockSpec(a.shape, lambda i: (0, 0))
    tile = lambda n: pl.BlockSpec((tm, n), lambda i: (i, 0))
    ttile = lambda n: pl.BlockSpec((n, tm), lambda i: (0, i))
    tok = lambda n, dt: (tile(n), jax.ShapeDtypeStruct((t, n), dt))
    tra = lambda n, dt: (ttile(n), jax.ShapeDtypeStruct((n, t), dt))
    outs = {'q': tok(FOX_WIDTH, BF16), 'vb': tok(FOX_WIDTH, BF16), 'gqkv': tok(3 * GDN_WIDTH, F32),
            'gz': tok(GDN_WIDTH, F32), 'gf': tok(D_MODEL, BF16), 'gg': tok(D_MODEL, BF16),
            'small': tok(LANES, F32), 'small_t': tra(N_SMALL, F32), 'ft': tra(N_SMALL, F32)}
    if kv_transposed:
        outs['ktf'] = tra(FOX_WIDTH, F32)
        outs['ktb'] = (pl.BlockSpec((tm // KEY_TILE, FOX_WIDTH, KEY_TILE), lambda i: (i, 0, 0)),
                       jax.ShapeDtypeStruct((t // KEY_TILE, FOX_WIDTH, KEY_TILE), BF16))
        outs['vtf'] = tra(FOX_WIDTH, F32)
    else:
        outs['kf'] = tok(FOX_WIDTH, F32)
        outs['kb'] = tok(FOX_WIDTH, BF16)
        outs['vf'] = tok(FOX_WIDTH, F32)
    res = pl.pallas_call(
        functools.partial(_proj_kernel, seg_len=seg_len, kv_transposed=kv_transposed),
        grid=(nt,),
        in_specs=[tile(d), mod_spec, mod_spec, const(wide), const(wsh), const(wsl), const(prow), const(pcol)],
        out_specs=[v[0] for v in outs.values()],
        out_shape=[v[1] for v in outs.values()],
        scratch_shapes=[pltpu.VMEM((N_SMALL, LANES), F32)],
        compiler_params=_params(("arbitrary",)),
        name="in_proj",
    )(x, shift, scale, wide, wsh, wsl, prow, pcol)
    return dict(zip(outs.keys(), res))


NEG_BIG = -1e30


def _fox_prompt_kernel(q_ref, kt_ref, v_ref, fk_ref, o_ref, qs_ref, m_ref, acc_ref, *, tk):
    qi = pl.program_id(1)
    tq = q_ref.shape[0]
    q = q_ref[...]
    lane = lax.broadcasted_iota(jnp.int32, q.shape, 1)
    lane_k = lax.broadcasted_iota(jnp.int32, (tk, LANES), 1)
    n_diag = tq // tk
    row = lax.broadcasted_iota(jnp.int32, (tq, tk), 0)
    col = lax.broadcasted_iota(jnp.int32, (tq, tk), 1)

    qs_ref[0:tq, :] = jnp.where(lane < FOX_HEAD_DIM, q, jnp.zeros_like(q))
    qs_ref[tq:2 * tq, :] = jnp.where(lane >= FOX_HEAD_DIM, q, jnp.zeros_like(q))
    m_ref[...] = jnp.full(m_ref.shape, NEG_BIG, F32)
    acc_ref[...] = jnp.zeros(acc_ref.shape, F32)

    def tile(j, masked_from):
        off = pl.multiple_of(j * tk, tk)
        v = v_ref[pl.ds(off, tk), :]
        s_both = _dot(qs_ref[...], kt_ref[j])
        for h in range(2):
            s = s_both[h * tq:(h + 1) * tq] - fk_ref[h, pl.ds(j, 1), :]
            if masked_from is not None:
                s = jnp.where(col + masked_from * tk <= row, s, NEG_BIG)
            vh = jnp.where((lane_k < FOX_HEAD_DIM) == (h == 0), v, jnp.ones_like(v))
            m_prev = m_ref[h]
            m_new = jnp.maximum(m_prev, jnp.max(s, axis=-1, keepdims=True))
            alpha = jnp.exp2(m_prev - m_new)
            p = jnp.concatenate([jnp.exp2(s[:, c * LANES:(c + 1) * LANES] - m_new)
                                 for c in range(tk // LANES)], axis=1).astype(BF16)
            acc_ref[h] = alpha * acc_ref[h] + _dot(p, vh)
            m_ref[h] = m_new

    def body(jj, c):
        for u in range(n_diag):
            tile(jj * n_diag + u, None)
        return c

    lax.fori_loop(0, qi, body, 0)
    for u in range(n_diag):
        tile(qi * n_diag + u, u)

    a0, a1 = acc_ref[0], acc_ref[1]
    o0 = a0 / pltpu.roll(a0, FOX_HEAD_DIM, axis=1)
    o1 = a1 / pltpu.roll(a1, FOX_HEAD_DIM, axis=1)
    o_ref[...] = jnp.where(lane < FOX_HEAD_DIM, o0, o1).astype(o_ref.dtype)


def _fox_prompt(q, kt, v, f_rows, *, tq):
    t = q.shape[0]
    nk, _, tk = kt.shape
    n_pairs = FOX_WIDTH // LANES
    fk = f_rows.reshape(n_pairs, 2, nk, tk)
    return pl.pallas_call(
        functools.partial(_fox_prompt_kernel, tk=tk),
        grid=(n_pairs, t // tq),
        in_specs=[pl.BlockSpec((tq, LANES), lambda p, i: (i, p)),
                  pl.BlockSpec((nk, LANES, tk), lambda p, i: (0, p, 0)),
                  pl.BlockSpec((t, LANES), lambda p, i: (0, p)),
                  pl.BlockSpec((None, 2, t // tk, tk), lambda p, i: (p, 0, 0, 0))],
        out_specs=pl.BlockSpec((tq, LANES), lambda p, i: (i, p)),
        out_shape=jax.ShapeDtypeStruct((t, FOX_WIDTH), BF16),
        scratch_shapes=[pltpu.VMEM((2 * tq, LANES), BF16), pltpu.VMEM((2, tq, LANES), F32),
                        pltpu.VMEM((2, tq, LANES), F32)],
        compiler_params=_params(("arbitrary", "arbitrary")),
        name="fox_prompt",
    )(q, kt, v, fk)


PAGES_PER_STEP = 8


def _suffix_kernel(lf_ref, m_ref, o_ref):
    l0, l1, l2 = _split3(lf_ref[...] * LOG2E)
    m = m_ref[...]
    o_ref[...] = _dot(l0, m) + (_dot(l1, m) + _dot(l2, m))


def _page_suffix(logf_t):
    n_phys = logf_t.shape[0]
    rows = n_phys * FOX_HEADS
    src = jnp.arange(PAGE_SIZE)
    mat = (src[:, None] >= src[None, :]).astype(BF16)
    br = 4096
    out = pl.pallas_call(
        _suffix_kernel,
        grid=(rows // br,),
        in_specs=[pl.BlockSpec((br, PAGE_SIZE), lambda i: (i, 0)),
                  pl.BlockSpec((PAGE_SIZE, PAGE_SIZE), lambda i: (0, 0))],
        out_specs=pl.BlockSpec((br, PAGE_SIZE), lambda i: (i, 0)),
        out_shape=jax.ShapeDtypeStruct((rows, PAGE_SIZE), F32),
        compiler_params=_params(("arbitrary",)),
        name="page_suffix",
    )(logf_t.reshape(rows, PAGE_SIZE), mat)
    return out.reshape(n_phys, FOX_HEADS, PAGE_SIZE)


SEQS_PER_STEP = 2


def _fox_sample_kernel(pt_ref, q_ref, kn_ref, vn_ref, fn_ref, suf_ref, *refs, n_pages):
    ns, gp = SEQS_PER_STEP, PAGES_PER_STEP
    k_refs = refs[0:ns * gp]
    v_refs = refs[ns * gp:2 * ns * gp]
    o_ref, qbd_ref, m_ref, l_ref, acc_ref, base_ref = refs[2 * ns * gp:]
    bi = pl.program_id(0)
    j = pl.program_id(1)
    nh, nq, hd = FOX_HEADS, q_ref.shape[1], FOX_HEAD_DIM
    rows = nh * nq
    seqs = range(ns)
    lane_w = lax.broadcasted_iota(jnp.int32, (rows, FOX_WIDTH), 1)
    row_w = lax.broadcasted_iota(jnp.int32, (rows, FOX_WIDTH), 0)
    own = (lane_w // hd) == (row_w // nq)

    @pl.when(j == 0)
    def _():
        for sq in seqs:
            qrep = jnp.broadcast_to(q_ref[sq][None], (nh, nq, FOX_WIDTH)).reshape(rows, FOX_WIDTH)
            qbd_ref[sq] = jnp.where(own, qrep, jnp.zeros_like(qrep))
        m_ref[...] = jnp.full(m_ref.shape, NEG_BIG, F32)
        l_ref[...] = jnp.zeros(l_ref.shape, F32)
        acc_ref[...] = jnp.zeros(acc_ref.shape, F32)
        base_ref[...] = jnp.zeros(base_ref.shape, F32)

    def update(scores, values, v_transposed):
        m_prev = [m_ref[sq] for sq in seqs]
        m_new = [jnp.maximum(m_prev[sq], jnp.max(scores[sq], axis=-1, keepdims=True)) for sq in seqs]
        p = [jnp.exp2(scores[sq] - m_new[sq]) for sq in seqs]
        pv = [(_dot_nt if v_transposed else _dot)(p[sq].astype(BF16), values[sq]) for sq in seqs]
        for sq in seqs:
            alpha = jnp.exp2(m_prev[sq] - m_new[sq])
            l_ref[sq] = alpha * l_ref[sq] + jnp.sum(p[sq], axis=-1, keepdims=True)
            acc_ref[sq] = alpha * acc_ref[sq] + pv[sq]
            m_ref[sq] = m_new[sq]

    lane_p = lax.broadcasted_iota(jnp.int32, (nh, PAGE_SIZE), 1)
    scores, values = [], []
    for sq in seqs:
        qbd = qbd_ref[sq]
        base = base_ref[sq]
        sc, va = [], []
        for g in range(gp):
            page = pt_ref[bi * ns + sq, n_pages - 1 - (j * gp + g)]
            incl = suf_ref[page]
            bias = jnp.where(lane_p < PAGE_SIZE - 1, pltpu.roll(incl, PAGE_SIZE - 1, axis=1), 0.0) + base
            base = base + incl[:, 0:1]
            kt = k_refs[sq * gp + g][...].reshape(FOX_WIDTH, PAGE_SIZE).astype(BF16)
            s = _dot(qbd, kt)
            sc.append((s.reshape(nh, nq, PAGE_SIZE) + bias[:, None, :]).reshape(rows, PAGE_SIZE))
            va.append(v_refs[sq * gp + g][...].reshape(FOX_WIDTH, PAGE_SIZE).astype(BF16))
        base_ref[sq] = base
        scores.append(jnp.concatenate(sc, axis=1))
        values.append(jnp.concatenate(va, axis=1))
    update(scores, values, True)

    @pl.when(j == pl.num_programs(1) - 1)
    def _():
        zpad = jnp.zeros((PAGE_SIZE - nq, FOX_WIDTH), BF16)
        qpos = lax.broadcasted_iota(jnp.int32, (nh, nq, PAGE_SIZE), 1)
        kpos = lax.broadcasted_iota(jnp.int32, (nh, nq, PAGE_SIZE), 2)
        sc, va = [], []
        for sq in seqs:
            kn = jnp.concatenate([kn_ref[sq], zpad], axis=0)
            s = _dot_nt(qbd_ref[sq], kn).reshape(nh, nq, PAGE_SIZE) - fn_ref[sq][:, None, :]
            sc.append(jnp.where(kpos <= qpos, s, NEG_BIG).reshape(rows, PAGE_SIZE))
            va.append(jnp.concatenate([vn_ref[sq], zpad], axis=0))
        update(sc, va, False)
        for sq in seqs:
            o = acc_ref[sq] / l_ref[sq]
            o = jnp.where(own, o, 0.0).reshape(nh, nq, FOX_WIDTH)
            o_ref[sq] = jnp.sum(o, axis=0).astype(o_ref.dtype)


def _fox_sample(page_table, q, k_new, v_new, f_new, cache_k, cache_v, suffix):
    b, t, w = q.shape
    n_pages = page_table.shape[1]
    ns, gp = SEQS_PER_STEP, PAGES_PER_STEP
    n_steps = n_pages // gp

    def page_map(sq, g):
        return lambda bi, j, pt: (pt[bi * ns + sq, n_pages - 1 - (j * gp + g)], 0, 0, 0)

    seq = lambda d1, d2: pl.BlockSpec((ns, d1, d2), lambda bi, j, pt: (bi, 0, 0))
    in_specs = [seq(t, w), seq(t, w), seq(t, w), seq(FOX_HEADS, PAGE_SIZE),
                pl.BlockSpec(suffix.shape, lambda bi, j, pt: (0, 0, 0), pipeline_mode=pl.Buffered(1))]
    kv_page = (None, FOX_HEADS, FOX_HEAD_DIM, PAGE_SIZE)
    pages = [(sq, g) for sq in range(ns) for g in range(gp)]
    in_specs += [pl.BlockSpec(kv_page, page_map(sq, g)) for sq, g in pages]
    in_specs += [pl.BlockSpec(kv_page, page_map(sq, g)) for sq, g in pages]
    rows = FOX_HEADS * t
    return pl.pallas_call(
        functools.partial(_fox_sample_kernel, n_pages=n_pages),
        grid_spec=pltpu.PrefetchScalarGridSpec(
            num_scalar_prefetch=1,
            grid=(b // ns, n_steps),
            in_specs=in_specs,
            out_specs=seq(t, w),
            scratch_shapes=[pltpu.VMEM((ns, rows, w), BF16), pltpu.VMEM((ns, rows, 1), F32),
                            pltpu.VMEM((ns, rows, 1), F32), pltpu.VMEM((ns, rows, w), F32),
                            pltpu.VMEM((ns, FOX_HEADS, PAGE_SIZE), F32)]),
        out_shape=jax.ShapeDtypeStruct((b, t, w), BF16),
        compiler_params=_params(("arbitrary", "arbitrary")),
        name="fox_sample",
    )(page_table, q, k_new, v_new, f_new, suffix, *([cache_k] * len(pages)), *([cache_v] * len(pages)))


GDN_CHUNK = 128
_HALO = 8


def _unit_lower_inverses(strict_lowers, row, col, n_active):
    c = min(strict_lowers[0].shape[0], pl.next_power_of_2(n_active))
    eye = jnp.where(row == col, 1.0, 0.0)
    base = (row // 2) == (col // 2)
    xs = [eye - jnp.where(base, lm, 0.0) for lm in strict_lowers]
    s = 2
    while s < c:
        off_mask = ((row // (2 * s)) == (col // (2 * s))) & ((row // s) != (col // s))
        split = [_split2(x) for x in xs]
        offs = [jnp.where(off_mask, lm, 0.0).astype(BF16) for lm in strict_lowers]
        ts = [(_dot(xh, off) + _dot(xl, off)).astype(BF16) for (xh, xl), off in zip(split, offs)]
        xs = [x - (_dot(t, xh) + _dot(t, xl)) for x, t, (xh, xl) in zip(xs, ts, split)]
        s *= 2
    return xs


def _gdn_kernel(u_ref, small_ref, smallt_ref, z_ref, s0_ref, conv0_ref, convw_ref, normw_ref,
                o_ref, sout_ref, s_scr, ext_scr, *, per_seq, n_active):
    i = pl.program_id(0)
    c = GDN_CHUNK
    hd = GDN_HEAD_DIM
    heads = range(GDN_HEADS)

    def load_state():
        s_scr[...] = s0_ref[...]
        ext_scr[0:_HALO, :] = conv0_ref[...]

    if per_seq:
        load_state()
    else:
        pl.when(i == 0)(load_state)

    u = u_ref[...]
    ext_scr[_HALO:_HALO + c, :] = u
    y = u * convw_ref[GDN_CONV - 1:GDN_CONV, :]
    for tap in range(GDN_CONV - 1):
        back = GDN_CONV - 1 - tap
        y = y + ext_scr[_HALO - back:_HALO - back + c, :] * convw_ref[tap:tap + 1, :]
    ext_scr[0:_HALO, :] = u[c - _HALO:c, :]
    y = _silu(y)

    row = lax.broadcasted_iota(jnp.int32, (c, c), 0)
    col = lax.broadcasted_iota(jnp.int32, (c, c), 1)
    incl = row >= col
    small = small_ref[...]
    tril = jnp.where(incl, 1.0, 0.0)
    triu = jnp.where(row <= col, 1.0, 0.0)
    tril = tril.astype(BF16)
    triu = triu.astype(BF16)
    c0, c1, c2 = _split3(small)
    gc_cols = _dot(tril, c0) + (_dot(tril, c1) + _dot(tril, c2))
    r0, r1, r2 = _split3(smallt_ref[...])
    gc_rows = _dot(r0, triu) + (_dot(r1, triu) + _dot(r2, triu))

    def head_cols(base, h):
        return y[:, base + h * hd:base + (h + 1) * hd]

    def unit(x):
        return x * lax.rsqrt(jnp.sum(x * x, axis=-1, keepdims=True) + NORM_EPS)

    q = [unit(head_cols(0, h)) * hd ** -0.5 for h in heads]
    k = [unit(head_cols(GDN_WIDTH, h)) for h in heads]
    v = [head_cols(2 * GDN_WIDTH, h) for h in heads]
    beta = [small[:, 12 + h:13 + h] for h in heads]
    gc = [gc_cols[:, 8 + h:9 + h] for h in heads]
    g_last = [g[c - 1:c, :] for g in gc]
    decay = [jnp.where(incl, jnp.exp(jnp.where(incl, gc[h] - gc_rows[8 + h:9 + h, :], 0.0)), 0.0) for h in heads]
    e_gc = [jnp.exp(g) for g in gc]
    kb = [k[h] * beta[h] for h in heads]
    k_b = [x.astype(BF16) for x in k]
    kb_s = [_split2(x) for x in kb]
    lmat = [jnp.where(row > col, (_dot_nt(kb_s[h][0], k_b[h]) + _dot_nt(kb_s[h][1], k_b[h])) * decay[h], 0.0)
            for h in heads]
    a_intra = [jnp.where(incl, _dot_nt(q[h].astype(BF16), k_b[h]) * decay[h], 0.0) for h in heads]
    tinv = [_split2(x) for x in _unit_lower_inverses(lmat, row, col, n_active)]
    vb = [(v[h] * beta[h]).astype(BF16) for h in heads]
    kg = [(kb[h] * e_gc[h]).astype(BF16) for h in heads]
    uu = [_dot(tinv[h][0], vb[h]) + _dot(tinv[h][1], vb[h]) for h in heads]
    ww = [(_dot(tinv[h][0], kg[h]) + _dot(tinv[h][1], kg[h])).astype(BF16) for h in heads]
    state = [s_scr[h] for h in heads]
    st_s = [_split2(x) for x in state]
    v_new = [uu[h] - (_dot(ww[h], st_s[h][0]) + _dot(ww[h], st_s[h][1])) for h in heads]
    o = [_dot((q[h] * e_gc[h]).astype(BF16), st_s[h][0]) + _dot(a_intra[h].astype(BF16), v_new[h].astype(BF16))
         for h in heads]
    for h in heads:
        kd = (k[h] * jnp.exp(g_last[h] - gc[h])).astype(BF16)
        vh, vl = _split2(v_new[h])
        s_scr[h] = state[h] * jnp.exp(g_last[h]) + (_dot_tn(kd, vh) + _dot_tn(kd, vl))
    for h in heads:
        sl = slice(h * hd, (h + 1) * hd)
        on = o[h] * lax.rsqrt(jnp.mean(o[h] * o[h], axis=-1, keepdims=True) + NORM_EPS) * normw_ref[...]
        o_ref[:, sl] = (on * _silu(z_ref[:, sl])).astype(o_ref.dtype)

    sout_ref[...] = s_scr[...]


def _gdn(u, small, small_t, z, s0, conv0, conv_w, norm_w, *, per_seq, n_active=GDN_CHUNK):
    c = GDN_CHUNK
    n = u.shape[0] // c
    nb = s0.shape[0]
    bsel = (lambda i: i) if per_seq else (lambda i: 0)
    return pl.pallas_call(
        functools.partial(_gdn_kernel, per_seq=per_seq, n_active=n_active),
        grid=(n,),
        in_specs=[pl.BlockSpec((c, 3 * GDN_WIDTH), lambda i: (i, 0)),
                  pl.BlockSpec((c, LANES), lambda i: (i, 0)),
                  pl.BlockSpec((16, c), lambda i: (0, i)),
                  pl.BlockSpec((c, GDN_WIDTH), lambda i: (i, 0)),
                  pl.BlockSpec((None, GDN_HEADS, GDN_HEAD_DIM, GDN_HEAD_DIM), lambda i: (bsel(i), 0, 0, 0)),
                  pl.BlockSpec((None, _HALO, 3 * GDN_WIDTH), lambda i: (bsel(i), 0, 0)),
                  pl.BlockSpec((8, 3 * GDN_WIDTH), lambda i: (0, 0)),
                  pl.BlockSpec((1, GDN_HEAD_DIM), lambda i: (0, 0))],
        out_specs=[pl.BlockSpec((c, GDN_WIDTH), lambda i: (i, 0)),
                   pl.BlockSpec((None, GDN_HEADS, GDN_HEAD_DIM, GDN_HEAD_DIM), lambda i: (bsel(i), 0, 0, 0))],
        out_shape=[jax.ShapeDtypeStruct((n * c, GDN_WIDTH), BF16),
                   jax.ShapeDtypeStruct((nb, GDN_HEADS, GDN_HEAD_DIM, GDN_HEAD_DIM), F32)],
        scratch_shapes=[pltpu.VMEM((GDN_HEADS, GDN_HEAD_DIM, GDN_HEAD_DIM), F32),
                        pltpu.VMEM((_HALO + c, 3 * GDN_WIDTH), F32)],
        compiler_params=_params(("arbitrary",)),
        name="gdn",
    )(u, small, small_t, z, s0, conv0, conv_w, norm_w)


_GROUP_LANE0 = N_EXPERTS


def _layer_norm(y, g, b):
    mu = jnp.mean(y, axis=-1, keepdims=True)
    d = y - mu
    var = jnp.mean(d * d, axis=-1, keepdims=True)
    return d * lax.rsqrt(var + LN_EPS) * g + b


def _pack_router(w_grp, b_grp, w_rt, b_rt):
    d = w_grp.shape[0]
    w = jnp.concatenate([w_rt, w_grp, jnp.zeros((d, LANES - N_EXPERTS - N_GROUPS), F32)], axis=1)
    b = jnp.concatenate([b_rt, b_grp, jnp.zeros((LANES - N_EXPERTS - N_GROUPS,), F32)]).reshape(1, LANES)
    wh = w.astype(BF16)
    wl = (w - wh.astype(F32)).astype(BF16)
    return wh, wl, b


def _mix_kernel(x_ref, of_ref, og_ref, gf_ref, gg_ref, wuf_ref, wug_ref, wo_ref, gate_ref, lng_ref, lnb_ref,
                shift_ref, scale_ref, wrh_ref, wrl_ref, br_ref,
                x1_ref, h2_ref, route_ref, sel_ref):
    merged = (gf_ref[...].astype(F32) * _dot(of_ref[...], wuf_ref[...])
              + gg_ref[...].astype(F32) * _dot(og_ref[...], wug_ref[...]))
    mix = _dot(merged.astype(BF16), wo_ref[...])
    x1 = _layer_norm(DEEPNORM_ALPHA * x_ref[...] + (1.0 + gate_ref[...]) * mix, lng_ref[...], lnb_ref[...])
    x1_ref[...] = x1
    h2 = x1 * (1.0 + scale_ref[...]) + shift_ref[...]
    h2_ref[...] = h2

    hh, hl = _split2(h2)
    wrh = wrh_ref[...]
    logits = _dot(hh, wrh) + (_dot(hh, wrl_ref[...]) + _dot(hl, wrh)) + br_ref[...]
    lane = lax.broadcasted_iota(jnp.int32, logits.shape, 1)
    big = jnp.int32(LANES)
    neg = jnp.float32(-jnp.inf)

    def top1(vals):
        m = jnp.max(vals, axis=-1, keepdims=True)
        idx = jnp.min(jnp.where(vals == m, lane, big), axis=-1, keepdims=True)
        return m, idx

    is_grp = (lane >= _GROUP_LANE0) & (lane < _GROUP_LANE0 + N_GROUPS)
    glog = jnp.where(is_grp, logits, neg)
    gmax, gidx = top1(glog)
    grp_p = 1.0 / jnp.sum(jnp.exp(glog - gmax), axis=-1, keepdims=True)
    grp = gidx - _GROUP_LANE0
    elog = jnp.where((lane // EXPERTS_PER_GROUP) == grp, logits, neg)
    m1, i1 = top1(elog)
    m2, i2 = top1(jnp.where(lane == i1, neg, elog))
    e21 = jnp.exp(m2 - m1)
    w1 = grp_p / (1.0 + e21)
    w2 = grp_p * e21 / (1.0 + e21)
    route = jnp.where(lane == 0, i1.astype(F32),
                      jnp.where(lane == 1, i2.astype(F32),
                                jnp.where(lane == 2, w1, jnp.where(lane == 3, w2, 0.0))))
    route_ref[...] = route
    sel_ref[...] = jnp.where((lane == i1) | (lane == i2), 1.0, 0.0).astype(BF16)


def _mix(x, o_fox, o_gdn, gf, gg, wuf, wug, wo, gate, ln_g, ln_b, shift, scale, wrh, wrl, br, *, tm):
    t, d = x.shape
    per_tok = gate.shape[0] != 1
    mod_spec = (pl.BlockSpec((tm, d), lambda i: (i, 0)) if per_tok
                else pl.BlockSpec((1, d), lambda i: (0, 0)))
    const = lambda a: pl.BlockSpec(a.shape, lambda i: (0, 0))
    tile = lambda n: pl.BlockSpec((tm, n), lambda i: (i, 0))
    return pl.pallas_call(
        _mix_kernel,
        grid=(t // tm,),
        in_specs=[tile(d), tile(FOX_WIDTH), tile(GDN_WIDTH), tile(d), tile(d), const(wuf), const(wug), const(wo),
                  mod_spec, const(ln_g), const(ln_b), mod_spec, mod_spec, const(wrh), const(wrl), const(br)],
        out_specs=[tile(d), tile(d), tile(LANES), tile(LANES)],
        out_shape=[jax.ShapeDtypeStruct((t, d), F32), jax.ShapeDtypeStruct((t, d), F32),
                   jax.ShapeDtypeStruct((t, LANES), F32), jax.ShapeDtypeStruct((t, LANES), BF16)],
        compiler_params=_params(("arbitrary",)),
        name="mix_ln_route",
    )(x, o_fox, o_gdn, gf, gg, wuf, wug, wo, gate, ln_g, ln_b, shift, scale, wrh, wrl, br)


MOE_ROWS = 512
ROW_DMA_TILE = 512


def _rank_kernel(sel_ref, route_ref, out_ref, cnt_ref, carry_ref):
    i = pl.program_id(0)
    tm = sel_ref.shape[0]

    @pl.when(i == 0)
    def _():
        carry_ref[...] = jnp.zeros_like(carry_ref)

    sel = sel_ref[...]
    row = lax.broadcasted_iota(jnp.int32, (tm, tm), 0)
    col = lax.broadcasted_iota(jnp.int32, (tm, tm), 1)
    strict = jnp.where(row > col, 1.0, 0.0).astype(BF16)
    rank = _dot(strict, sel) + carry_ref[0:1, :]
    lane = lax.broadcasted_iota(jnp.int32, rank.shape, 1)
    route = route_ref[...]
    e0 = route[:, 0:1].astype(jnp.int32)
    e1 = route[:, 1:2].astype(jnp.int32)
    r0 = jnp.sum(jnp.where(lane == e0, rank, 0.0), axis=-1, keepdims=True)
    r1 = jnp.sum(jnp.where(lane == e1, rank, 0.0), axis=-1, keepdims=True)
    out_ref[...] = jnp.where(lane == 0, r0, jnp.where(lane == 1, r1, 0.0))
    total = rank[tm - 1:tm, :] + sel[tm - 1:tm, :].astype(F32)
    carry_ref[0:1, :] = total
    cnt_ref[...] = jnp.broadcast_to(total, cnt_ref.shape)


def _rank(sel, route, *, tm):
    t = sel.shape[0]
    tile = pl.BlockSpec((tm, LANES), lambda i: (i, 0))
    return pl.pallas_call(
        _rank_kernel,
        grid=(t // tm,),
        in_specs=[tile, tile],
        out_specs=[tile, pl.BlockSpec((8, LANES), lambda i: (0, 0))],
        out_shape=[jax.ShapeDtypeStruct((t, LANES), F32), jax.ShapeDtypeStruct((8, LANES), F32)],
        scratch_shapes=[pltpu.VMEM((8, LANES), F32)],
        compiler_params=_params(("arbitrary",)),
        name="moe_rank",
    )(sel, route)


def _scatter_kernel(pos_ref, h_ref, init_ref, xs_ref, sem, *, tok0, n_tok):
    del init_ref
    i = pl.program_id(0)
    tm = h_ref.shape[0]

    def copy(r, slot):
        dst = pos_ref[slot * n_tok + tok0 + i * tm + r]
        return pltpu.make_async_copy(h_ref.at[pl.ds(r, 1), :], xs_ref.at[pl.ds(dst, 1), :], sem)

    def start(r, c):
        copy(r, 0).start(priority=0)
        copy(r, 1).start(priority=1)
        return c

    def wait(r, c):
        copy(r, 0).wait()
        copy(r, 1).wait()
        return c

    lax.fori_loop(0, tm, start, 0, unroll=8)
    lax.fori_loop(0, tm, wait, 0, unroll=8)


def _scatter_rows(pos, h, init, *, tm, tok0):
    t, d = h.shape
    n_sorted = init.shape[0]
    return pl.pallas_call(
        functools.partial(_scatter_kernel, tok0=tok0, n_tok=pos.shape[0] // 2),
        grid_spec=pltpu.PrefetchScalarGridSpec(
            num_scalar_prefetch=1,
            grid=(t // tm,),
            in_specs=[pl.BlockSpec((tm, d), lambda i, pos: (i, 0)),
                      pl.BlockSpec(memory_space=pl.ANY)],
            out_specs=pl.BlockSpec(memory_space=pl.ANY),
            scratch_shapes=[pltpu.SemaphoreType.DMA(())]),
        out_shape=jax.ShapeDtypeStruct((n_sorted, d), F32),
        input_output_aliases={2: 0},
        compiler_params=_params(("arbitrary",)),
        name="moe_scatter",
    )(pos, h, init)


def _experts_kernel(te_ref, x_ref, wg_ref, wu_ref, wd_ref, y_ref, wgb_ref, wub_ref, wdb_ref):
    i = pl.program_id(0)

    @pl.when((i == 0) | (te_ref[0, i] != te_ref[0, jnp.maximum(i - 1, 0)]))
    def _():
        wgb_ref[...] = wg_ref[...].astype(BF16)
        wub_ref[...] = wu_ref[...].astype(BF16)
        wdb_ref[...] = wd_ref[...].astype(BF16)

    @pl.when(te_ref[1, i] > 0)
    def _():
        xb = x_ref[...].astype(BF16)
        act = _silu(_dot(xb, wgb_ref[...])) * _dot(xb, wub_ref[...])
        y_ref[...] = _dot(act.astype(BF16), wdb_ref[...])

    @pl.when(te_ref[1, i] == 0)
    def _():
        y_ref[...] = jnp.zeros_like(y_ref)


def _experts(tile_info, xs, w_g, w_u, w_d):
    n_sorted, d = xs.shape
    ff = w_g.shape[2]
    n_tiles = n_sorted // MOE_ROWS
    return pl.pallas_call(
        _experts_kernel,
        grid_spec=pltpu.PrefetchScalarGridSpec(
            num_scalar_prefetch=1,
            grid=(n_tiles,),
            in_specs=[pl.BlockSpec((MOE_ROWS, d), lambda i, te: (i, 0)),
                      pl.BlockSpec((None, d, ff), lambda i, te: (te[0, i], 0, 0)),
                      pl.BlockSpec((None, d, ff), lambda i, te: (te[0, i], 0, 0)),
                      pl.BlockSpec((None, ff, d), lambda i, te: (te[0, i], 0, 0))],
            out_specs=pl.BlockSpec((MOE_ROWS, d), lambda i, te: (i, 0)),
            scratch_shapes=[pltpu.VMEM((d, ff), BF16), pltpu.VMEM((d, ff), BF16), pltpu.VMEM((ff, d), BF16)]),
        out_shape=jax.ShapeDtypeStruct((n_sorted, d), F32),
        compiler_params=_params(("arbitrary",)),
        name="moe_experts",
    )(tile_info, xs, w_g, w_u, w_d)


def _combine_kernel(pos_ref, ys_ref, x1_ref, route_ref, gate_ref, lng_ref, lnb_ref, out_ref, buf, sem,
                    *, tok0, n_tok):
    i = pl.program_id(0)
    tm = x1_ref.shape[0]

    def copy(r, slot):
        src = pos_ref[slot * n_tok + tok0 + i * tm + r]
        return pltpu.make_async_copy(ys_ref.at[pl.ds(src, 1), :], buf.at[slot, pl.ds(r, 1), :], sem)

    def start(r, c):
        copy(r, 0).start(priority=0)
        copy(r, 1).start(priority=1)
        return c

    def wait(r, c):
        copy(r, 0).wait()
        copy(r, 1).wait()
        return c

    lax.fori_loop(0, tm, start, 0, unroll=8)
    lax.fori_loop(0, tm, wait, 0, unroll=8)
    route = route_ref[...]
    ffn = route[:, 2:3] * buf[0] + route[:, 3:4] * buf[1]
    y = DEEPNORM_ALPHA * x1_ref[...] + (1.0 + gate_ref[...]) * ffn
    out_ref[...] = _layer_norm(y, lng_ref[...], lnb_ref[...])


def _combine(pos, ys, x1, route, gate, ln_g, ln_b, *, tm, tok0):
    t, d = x1.shape
    gate_spec = (pl.BlockSpec((tm, d), lambda i, pos: (i, 0)) if gate.shape[0] != 1
                 else pl.BlockSpec((1, d), lambda i, pos: (0, 0)))
    return pl.pallas_call(
        functools.partial(_combine_kernel, tok0=tok0, n_tok=pos.shape[0] // 2),
        grid_spec=pltpu.PrefetchScalarGridSpec(
            num_scalar_prefetch=1,
            grid=(t // tm,),
            in_specs=[pl.BlockSpec(memory_space=pl.ANY),
                      pl.BlockSpec((tm, d), lambda i, pos: (i, 0)),
                      pl.BlockSpec((tm, LANES), lambda i, pos: (i, 0)),
                      gate_spec,
                      pl.BlockSpec((1, d), lambda i, pos: (0, 0)),
                      pl.BlockSpec((1, d), lambda i, pos: (0, 0))],
            out_specs=pl.BlockSpec((tm, d), lambda i, pos: (i, 0)),
            scratch_shapes=[pltpu.VMEM((2, tm, d), F32), pltpu.SemaphoreType.DMA(())]),
        out_shape=jax.ShapeDtypeStruct((t, d), F32),
        compiler_params=_params(("arbitrary",)),
        name="moe_combine_ln",
    )(pos, ys, x1, route, gate, ln_g, ln_b)


def _moe(groups, w_g, w_u, w_d, ln_g, ln_b, *, tm):
    route = jnp.concatenate([g[2] for g in groups], axis=0)
    sel = jnp.concatenate([g[3] for g in groups], axis=0)
    n = route.shape[0]
    ranks, counts = _rank(sel, route, tm=tm)
    counts = counts[0, :N_EXPERTS].astype(jnp.int32)
    padded = ((counts + MOE_ROWS - 1) // MOE_ROWS) * MOE_ROWS
    ends = jnp.cumsum(padded)
    offsets = ends - padded
    n_tiles = (2 * n) // MOE_ROWS + N_EXPERTS
    tile_start = jnp.arange(n_tiles, dtype=jnp.int32) * MOE_ROWS
    tile_expert = jnp.sum((tile_start[:, None] >= ends[None, :]).astype(jnp.int32), axis=1)
    tile_expert = jnp.minimum(tile_expert, N_EXPERTS - 1)
    tile_used = (tile_start < ends[-1]).astype(jnp.int32)
    tile_info = jnp.stack([tile_expert, tile_used])
    e = route[:, 0:2].astype(jnp.int32)
    r = ranks[:, 0:2].astype(jnp.int32)
    pos = (offsets[e] + r).T.reshape(-1)
    xs = jnp.zeros((n_tiles * MOE_ROWS, h2_dim(groups)), F32)
    tok0 = 0
    for h2, _, _, _, _ in groups:
        xs = _scatter_rows(pos, h2, xs, tm=_dma_tile(h2.shape[0]), tok0=tok0)
        tok0 += h2.shape[0]
    ys = _experts(tile_info, xs, w_g, w_u, w_d)
    outs = []
    tok0 = 0
    for _, x1, rt, _, gate in groups:
        outs.append(_combine(pos, ys, x1, rt, gate, ln_g, ln_b, tm=_dma_tile(x1.shape[0]), tok0=tok0))
        tok0 += x1.shape[0]
    return outs


def h2_dim(groups):
    return groups[0][0].shape[1]


def _dma_tile(n_rows):
    return ROW_DMA_TILE if n_rows % ROW_DMA_TILE == 0 else n_rows


TOKEN_TILE = 256
ATTN_TILE = 512
QUERY_TILE = 1024


def kernel(x_prompt, x_sample, cache_fox_k, cache_fox_v, cache_fox_logf, state_gdn, state_gdn_conv, page_table,
           c_prompt, c_sample, w_ada_mix, b_ada_mix, w_in, b_forget, gdn_conv_w, gdn_a_log, gdn_dt_bias,
           gdn_norm_w, w_up_fox, w_up_gdn, w_out, ln1_g, ln1_b, w_ada_ffn, b_ada_ffn, w_group_router,
           b_group_router, w_expert_router, b_expert_router, w_expert_gate, w_expert_up, w_expert_down,
           ln2_g, ln2_b):
    assert w_in.shape[0] == 1, "single-layer step"
    bp, tp, d = x_prompt.shape
    bs, ts, _ = x_sample.shape
    assert bp == 1
    tm = TOKEN_TILE
    row = lambda a: a.reshape(1, -1)

    n_c = bp + bs
    c_all = jnp.concatenate([c_prompt, c_sample, jnp.zeros((-n_c % 8, d), F32)], axis=0)
    mod_mix = _adaln(c_all, w_ada_mix[0], b_ada_mix[0])
    mod_ffn = _adaln(c_all, w_ada_ffn[0], b_ada_ffn[0])

    def split_mod(mod, lo, hi, rep):
        m = mod[lo:hi]
        if rep > 1:
            m = jnp.repeat(m, rep, axis=0)
        return m[:, 0:d], m[:, d:2 * d], m[:, 2 * d:3 * d]

    wide, wsh, wsl = _pack_w_in(jnp.transpose(w_in[0]))
    prow, pcol = _param_rows(b_forget[0], gdn_a_log[0], gdn_dt_bias[0])
    conv_w = jnp.pad(gdn_conv_w[0], ((0, 8 - GDN_CONV), (0, 0)))
    norm_w = row(gdn_norm_w[0])
    wuf, wug, wo = w_up_fox[0].astype(BF16), w_up_gdn[0].astype(BF16), w_out[0].astype(BF16)
    wrh, wrl, br = _pack_router(w_group_router[0], b_group_router[0], w_expert_router[0], b_expert_router[0])
    ln1 = (row(ln1_g[0]), row(ln1_b[0]))
    heads = (FOX_HEADS, FOX_HEAD_DIM)

    shift1, scale1, gate1 = split_mod(mod_mix, 0, 1, 1)
    shift2, scale2, gate2_p = split_mod(mod_ffn, 0, 1, 1)
    xp = x_prompt[0]
    pr = _project(xp, shift1, scale1, wide, wsh, wsl, prow, pcol, tm=ATTN_TILE, seg_len=None, kv_transposed=True)
    o_fox = _fox_prompt(pr['q'], pr['ktb'], pr['vb'], pr['ft'][0:FOX_HEADS], tq=QUERY_TILE)
    o_gdn, state_p = _gdn(pr['gqkv'], pr['small'], pr['small_t'], pr['gz'],
                          jnp.zeros((1, GDN_HEADS, GDN_HEAD_DIM, GDN_HEAD_DIM), F32),
                          jnp.zeros((1, _HALO, 3 * GDN_WIDTH), F32), conv_w, norm_w, per_seq=False)
    x1_p, h2_p, route_p, sel_p = _mix(xp, o_fox, o_gdn, pr['gf'], pr['gg'], wuf, wug, wo, gate1, *ln1,
                                      shift2, scale2, wrh, wrl, br, tm=tm)
    head_major = lambda a: jnp.transpose(a.reshape(*heads, tp), (2, 0, 1))
    prompt_rows = (head_major(pr['ktf']), head_major(pr['vtf']), jnp.transpose(pr['small_t'][0:FOX_HEADS]),
                   state_p, pr['gqkv'][tp - (GDN_CONV - 1):])

    n_s = bs * ts
    shift1, scale1, gate1 = split_mod(mod_mix, 1, 1 + bs, ts)
    shift2, scale2, gate2_s = split_mod(mod_ffn, 1, 1 + bs, ts)
    xs = x_sample.reshape(n_s, d)
    sr = _project(xs, shift1, scale1, wide, wsh, wsl, prow, pcol, tm=n_s, seg_len=ts, kv_transposed=False)
    suffix = _page_suffix(jnp.transpose(cache_fox_logf[0], (0, 2, 1)))
    lanes_per_seq = lambda a: a.reshape(a.shape[0], bs, ts)
    f_new = jnp.pad(jnp.transpose(lanes_per_seq(sr['ft'][0:FOX_HEADS]), (1, 0, 2)),
                    ((0, 0), (0, 0), (0, PAGE_SIZE - ts)))
    seq3 = lambda a: a.reshape(bs, ts, a.shape[-1])
    o_fox = _fox_sample(page_table, seq3(sr['q']), seq3(sr['kb']), seq3(sr['vb']), f_new,
                        jnp.transpose(cache_fox_k[0], (0, 2, 3, 1)), jnp.transpose(cache_fox_v[0], (0, 2, 3, 1)),
                        suffix).reshape(n_s, FOX_WIDTH)
    chunk = lambda a: jnp.pad(seq3(a), ((0, 0), (0, GDN_CHUNK - ts), (0, 0))).reshape(bs * GDN_CHUNK, a.shape[-1])
    small_t_c = jnp.pad(lanes_per_seq(sr['small_t']), ((0, 0), (0, 0), (0, GDN_CHUNK - ts))).reshape(N_SMALL, -1)
    conv0 = jnp.pad(state_gdn_conv[0], ((0, 0), (_HALO - (GDN_CONV - 1), 0), (0, 0)))
    o_gdn, state_s = _gdn(chunk(sr['gqkv']), chunk(sr['small']), small_t_c, chunk(sr['gz']), state_gdn[0], conv0,
                          conv_w, norm_w, per_seq=True, n_active=ts)
    o_gdn = o_gdn.reshape(bs, GDN_CHUNK, GDN_WIDTH)[:, 0:ts].reshape(n_s, GDN_WIDTH)
    x1_s, h2_s, route_s, sel_s = _mix(xs, o_fox, o_gdn, sr['gf'], sr['gg'], wuf, wug, wo, gate1, *ln1,
                                      shift2, scale2, wrh, wrl, br, tm=n_s)
    sample_rows = (sr['kf'], sr['vf'], sr['small'][:, 0:FOX_HEADS], state_s,
                   seq3(sr['gqkv'])[:, ts - (GDN_CONV - 1):])

    y_p, y_s = _moe([(h2_p, x1_p, route_p, sel_p, gate2_p), (h2_s, x1_s, route_s, sel_s, gate2_s)],
                    w_expert_gate[0], w_expert_up[0], w_expert_down[0], row(ln2_g[0]), row(ln2_b[0]), tm=tm)

    kp, vp, lfp, sp, cvp = prompt_rows
    ks, vs, lfs, ss, cvs = sample_rows
    return (y_p.reshape(1, tp, d), y_s.reshape(bs, ts, d),
            kp.reshape(1, 1, tp, *heads), vp.reshape(1, 1, tp, *heads), lfp.reshape(1, 1, tp, FOX_HEADS),
            sp.reshape(1, 1, GDN_HEADS, GDN_HEAD_DIM, GDN_HEAD_DIM), cvp.reshape(1, 1, GDN_CONV - 1, 3 * GDN_WIDTH),
            ks.reshape(1, bs, ts, *heads), vs.reshape(1, bs, ts, *heads), lfs.reshape(1, bs, ts, FOX_HEADS),
            ss.reshape(1, bs, GDN_HEADS, GDN_HEAD_DIM, GDN_HEAD_DIM),
            cvs.reshape(1, bs, GDN_CONV - 1, 3 * GDN_WIDTH))
```

```python
import functools

import jax
import jax.numpy as jnp
from jax import lax
from jax.experimental import pallas as pl
from jax.experimental.pallas import tpu as pltpu

F32 = jnp.float32
BF16 = jnp.bfloat16

D_MODEL = 1024
FOX_HEADS = 8
FOX_HEAD_DIM = 64
FOX_WIDTH = FOX_HEADS * FOX_HEAD_DIM
GDN_HEADS = 4
GDN_HEAD_DIM = 128
GDN_WIDTH = GDN_HEADS * GDN_HEAD_DIM
GDN_CONV = 4
N_GROUPS = 4
EXPERTS_PER_GROUP = 8
N_EXPERTS = N_GROUPS * EXPERTS_PER_GROUP
EXPERT_FF = 512
PAGE_SIZE = 128
DEEPNORM_ALPHA = 2.0 ** 0.25
LN_EPS = 1e-5
NORM_EPS = 1e-6
LOG2E = 1.4426950408889634
LANES = 128
VMEM_LIMIT = 56 * 1024 * 1024

_C_FQ, _C_FK, _C_FV, _C_GQKV, _C_GZ, _C_GF, _C_GG, _C_END = 0, 512, 1024, 1536, 3072, 3584, 4608, 5632


def _params(sem):
    return pltpu.CompilerParams(dimension_semantics=sem, vmem_limit_bytes=VMEM_LIMIT)


def _dot(a, b):
    return jnp.dot(a, b, preferred_element_type=F32)


def _dot_nt(a, b):
    return lax.dot_general(a, b, (((1,), (1,)), ((), ())), preferred_element_type=F32)


def _dot_tn(a, b):
    return lax.dot_general(a, b, (((0,), (0,)), ((), ())), preferred_element_type=F32)


def _split2(x):
    hi = x.astype(BF16)
    lo = (x - hi.astype(F32)).astype(BF16)
    return hi, lo


def _split3(x):
    a = x.astype(BF16)
    r = x - a.astype(F32)
    b = r.astype(BF16)
    c = (r - b.astype(F32)).astype(BF16)
    return a, b, c


def _dot3(a, b):
    ah, al = _split2(a)
    bh, bl = _split2(b)
    return _dot(ah, bh) + (_dot(ah, bl) + _dot(al, bh))


def _dot3_nt(a, b):
    ah, al = _split2(a)
    bh, bl = _split2(b)
    return _dot_nt(ah, bh) + (_dot_nt(ah, bl) + _dot_nt(al, bh))


def _silu(x):
    return x * jax.nn.sigmoid(x)


def _softplus(x):
    return jnp.maximum(x, 0.0) + jnp.log1p(jnp.exp(-jnp.abs(x)))


def _log_sigmoid(x):
    return jnp.minimum(x, 0.0) - jnp.log1p(jnp.exp(-jnp.abs(x)))


def _ada_kernel(c_ref, w_ref, b_ref, o_ref):
    s = _silu(c_ref[...])
    o_ref[...] = jnp.dot(s, w_ref[...], preferred_element_type=F32,
                         precision=lax.Precision.HIGHEST) + b_ref[...]


def _adaln(c, w, b):
    rows, d = c.shape
    n = w.shape[1]
    bn = 512
    return pl.pallas_call(
        _ada_kernel,
        grid=(n // bn,),
        in_specs=[pl.BlockSpec((rows, d), lambda j: (0, 0)),
                  pl.BlockSpec((d, bn), lambda j: (0, j)),
                  pl.BlockSpec((1, bn), lambda j: (0, j))],
        out_specs=pl.BlockSpec((rows, bn), lambda j: (0, j)),
        out_shape=jax.ShapeDtypeStruct((rows, n), F32),
        compiler_params=_params(("arbitrary",)),
        name="adaln",
    )(c, w, b.reshape(1, n))


KEY_TILE = 256
N_SMALL = 16


def _pack_w_in(w_in_t):
    wide = jnp.concatenate([w_in_t[0:1536], w_in_t[1544:3592], w_in_t[3600:5648]], axis=0).astype(BF16)
    small = jnp.concatenate([w_in_t[1536:1544], w_in_t[3592:3600],
                             jnp.zeros((LANES - N_SMALL, w_in_t.shape[1]), F32)], axis=0)
    sh = small.astype(BF16)
    sl = (small - sh.astype(F32)).astype(BF16)
    return wide, sh, sl


def _param_rows(b_forget, a_log, dt_bias):
    z = jnp.zeros((LANES,), F32)
    bias = z.at[0:8].set(b_forget).at[8:12].set(dt_bias)
    nega = z.at[8:12].set(-jnp.exp(a_log))
    rows = jnp.zeros((8, LANES), F32).at[0].set(bias).at[1].set(nega)
    cols = jnp.zeros((N_SMALL, LANES), F32).at[:, 0].set(bias[0:N_SMALL]).at[:, 1].set(nega[0:N_SMALL])
    return rows, cols


def _small_act(z, idx, nega):
    return jnp.where(idx < 8, _log_sigmoid(z), jnp.where(idx < 12, nega * _softplus(z), jax.nn.sigmoid(z)))


def _proj_kernel(x_ref, shift_ref, scale_ref, w_ref, wsh_ref, wsl_ref, prow_ref, pcol_ref, *refs,
                 seg_len, kv_transposed):
    if kv_transposed:
        (q_ref, vb_ref, gqkv_ref, gz_ref, gf_ref, gg_ref, small_ref, smallt_ref, ft_ref,
         ktf_ref, ktb_ref, vtf_ref, carry_ref) = refs
    else:
        (q_ref, vb_ref, gqkv_ref, gz_ref, gf_ref, gg_ref, small_ref, smallt_ref, ft_ref,
         kf_ref, kb_ref, vf_ref, carry_ref) = refs
    i = pl.program_id(0)
    tm = x_ref.shape[0]

    @pl.when(i == 0)
    def _():
        carry_ref[...] = jnp.zeros_like(carry_ref)

    h = x_ref[...] * (1.0 + scale_ref[...]) + shift_ref[...]
    hb, hl = _split2(h)
    seg = lambda lo, hi: w_ref[lo:hi, :]

    q_ref[...] = (_dot_nt(hb, seg(_C_FQ, _C_FK)) * (LOG2E * FOX_HEAD_DIM ** -0.5)).astype(BF16)
    v = _dot_nt(hb, seg(_C_FV, _C_GQKV))
    vb_ref[...] = v.astype(BF16)
    if kv_transposed:
        kt = _dot_nt(seg(_C_FK, _C_FV), hb)
        ktf_ref[...] = kt
        ktbf = kt.astype(BF16)
        for c in range(ktb_ref.shape[0]):
            ktb_ref[c] = ktbf[:, c * KEY_TILE:(c + 1) * KEY_TILE]
        vtf_ref[...] = v.T
    else:
        k = _dot_nt(hb, seg(_C_FK, _C_FV))
        kf_ref[...] = k
        kb_ref[...] = k.astype(BF16)
        vf_ref[...] = v
    gqkv_ref[...] = _dot_nt(hb, seg(_C_GQKV, _C_GZ))
    gz_ref[...] = _dot_nt(hb, seg(_C_GZ, _C_GF))
    gf_ref[...] = jax.nn.sigmoid(_dot_nt(hb, seg(_C_GF, _C_GG))).astype(BF16)
    gg_ref[...] = jax.nn.sigmoid(_dot_nt(hb, seg(_C_GG, _C_END))).astype(BF16)

    wsh = wsh_ref[...]
    z = _dot_nt(hb, wsh) + (_dot_nt(hb, wsl_ref[...]) + _dot_nt(hl, wsh)) + prow_ref[0:1, :]
    lane = lax.broadcasted_iota(jnp.int32, z.shape, 1)
    small_ref[...] = _small_act(z, lane, prow_ref[1:2, :])

    ws16h = wsh_ref[0:N_SMALL, :]
    zt = (_dot_nt(ws16h, hb) + (_dot_nt(wsl_ref[0:N_SMALL, :], hb) + _dot_nt(ws16h, hl))) + pcol_ref[:, 0:1]
    rid = lax.broadcasted_iota(jnp.int32, zt.shape, 0)
    small_t = _small_act(zt, rid, pcol_ref[:, 1:2])
    smallt_ref[...] = small_t
    row = lax.broadcasted_iota(jnp.int32, (tm, tm), 0)
    col = lax.broadcasted_iota(jnp.int32, (tm, tm), 1)
    keep = row <= col
    if seg_len is not None:
        keep = keep & ((row // seg_len) == (col // seg_len))
    tri = jnp.where(keep, 1.0, 0.0).astype(BF16)
    l0, l1, l2 = _split3(jnp.where(rid < FOX_HEADS, small_t, 0.0))
    fcum = _dot(l0, tri) + (_dot(l1, tri) + _dot(l2, tri))
    if seg_len is None:
        fcum = fcum + carry_ref[:, 0:1]
        carry_ref[...] = jnp.broadcast_to(fcum[:, tm - 1:tm], carry_ref.shape)
    ft_ref[...] = fcum * LOG2E


def _project(x, shift, scale, wide, wsh, wsl, prow, pcol, *, tm, seg_len, kv_transposed):
    t, d = x.shape
    nt = t // tm
    per_tok = shift.shape[0] != 1
    mod_spec = (pl.BlockSpec((tm, d), lambda i: (i, 0)) if per_tok
                else pl.BlockSpec((1, d), lambda i: (0, 0)))
    const = lambda a: pl.BlockSpec(a.shape, lambda i: (0, 0))
    tile = lambda n: pl.BlockSpec((tm, n), lambda i: (i, 0))
    ttile = lambda n: pl.BlockSpec((n, tm), lambda i: (0, i))
    tok = lambda n, dt: (tile(n), jax.ShapeDtypeStruct((t, n), dt))
    tra = lambda n, dt: (ttile(n), jax.ShapeDtypeStruct((n, t), dt))
    outs = {'q': tok(FOX_WIDTH, BF16), 'vb': tok(FOX_WIDTH, BF16), 'gqkv': tok(3 * GDN_WIDTH, F32),
            'gz': tok(GDN_WIDTH, F32), 'gf': tok(D_MODEL, BF16), 'gg': tok(D_MODEL, BF16),
            'small': tok(LANES, F32), 'small_t': tra(N_SMALL, F32), 'ft': tra(N_SMALL, F32)}
    if kv_transposed:
        outs['ktf'] = tra(FOX_WIDTH, F32)
        outs['ktb'] = (pl.BlockSpec((tm // KEY_TILE, FOX_WIDTH, KEY_TILE), lambda i: (i, 0, 0)),
                       jax.ShapeDtypeStruct((t // KEY_TILE, FOX_WIDTH, KEY_TILE), BF16))
        outs['vtf'] = tra(FOX_WIDTH, F32)
    else:
        outs['kf'] = tok(FOX_WIDTH, F32)
        outs['kb'] = tok(FOX_WIDTH, BF16)
        outs['vf'] = tok(FOX_WIDTH, F32)
    res = pl.pallas_call(
        functools.partial(_proj_kernel, seg_len=seg_len, kv_transposed=kv_transposed),
        grid=(nt,),
        in_specs=[tile(d), mod_spec, mod_spec, const(wide), const(wsh), const(wsl), const(prow), const(pcol)],
        out_specs=[v[0] for v in outs.values()],
        out_shape=[v[1] for v in outs.values()],
        scratch_shapes=[pltpu.VMEM((N_SMALL, LANES), F32)],
        compiler_params=_params(("arbitrary",)),
        name="in_proj",
    )(x, shift, scale, wide, wsh, wsl, prow, pcol)
    return dict(zip(outs.keys(), res))


NEG_BIG = -1e30


def _fox_prompt_kernel(q_ref, kt_ref, v_ref, fk_ref, o_ref, qs_ref, m_ref, acc_ref, *, tk):
    qi = pl.program_id(1)
    tq = q_ref.shape[0]
    q = q_ref[...]
    lane = lax.broadcasted_iota(jnp.int32, q.shape, 1)
    lane_k = lax.broadcasted_iota(jnp.int32, (tk, LANES), 1)
    n_diag = tq // tk
    row = lax.broadcasted_iota(jnp.int32, (tq, tk), 0)
    col = lax.broadcasted_iota(jnp.int32, (tq, tk), 1)

    qs_ref[0:tq, :] = jnp.where(lane < FOX_HEAD_DIM, q, jnp.zeros_like(q))
    qs_ref[tq:2 * tq, :] = jnp.where(lane >= FOX_HEAD_DIM, q, jnp.zeros_like(q))
    m_ref[...] = jnp.full(m_ref.shape, NEG_BIG, F32)
    acc_ref[...] = jnp.zeros(acc_ref.shape, F32)

    def tile(j, masked_from):
        off = pl.multiple_of(j * tk, tk)
        v = v_ref[pl.ds(off, tk), :]
        s_both = _dot(qs_ref[...], kt_ref[j])
        for h in range(2):
            s = s_both[h * tq:(h + 1) * tq] - fk_ref[h, pl.ds(j, 1), :]
            if masked_from is not None:
                s = jnp.where(col + masked_from * tk <= row, s, NEG_BIG)
            vh = jnp.where((lane_k < FOX_HEAD_DIM) == (h == 0), v, jnp.ones_like(v))
            m_prev = m_ref[h]
            m_new = jnp.maximum(m_prev, jnp.max(s, axis=-1, keepdims=True))
            alpha = jnp.exp2(m_prev - m_new)
            p = jnp.concatenate([jnp.exp2(s[:, c * LANES:(c + 1) * LANES] - m_new)
                                 for c in range(tk // LANES)], axis=1).astype(BF16)
            acc_ref[h] = alpha * acc_ref[h] + _dot(p, vh)
            m_ref[h] = m_new

    def body(jj, c):
        for u in range(n_diag):
            tile(jj * n_diag + u, None)
        return c

    lax.fori_loop(0, qi, body, 0)
    for u in range(n_diag):
        tile(qi * n_diag + u, u)

    a0, a1 = acc_ref[0], acc_ref[1]
    o0 = a0 / pltpu.roll(a0, FOX_HEAD_DIM, axis=1)
    o1 = a1 / pltpu.roll(a1, FOX_HEAD_DIM, axis=1)
    o_ref[...] = jnp.where(lane < FOX_HEAD_DIM, o0, o1).astype(o_ref.dtype)


def _fox_prompt(q, kt, v, f_rows, *, tq):
    t = q.shape[0]
    nk, _, tk = kt.shape
    n_pairs = FOX_WIDTH // LANES
    fk = f_rows.reshape(n_pairs, 2, nk, tk)
    return pl.pallas_call(
        functools.partial(_fox_prompt_kernel, tk=tk),
        grid=(n_pairs, t // tq),
        in_specs=[pl.BlockSpec((tq, LANES), lambda p, i: (i, p)),
                  pl.BlockSpec((nk, LANES, tk), lambda p, i: (0, p, 0)),
                  pl.BlockSpec((t, LANES), lambda p, i: (0, p)),
                  pl.BlockSpec((None, 2, t // tk, tk), lambda p, i: (p, 0, 0, 0))],
        out_specs=pl.BlockSpec((tq, LANES), lambda p, i: (i, p)),
        out_shape=jax.ShapeDtypeStruct((t, FOX_WIDTH), BF16),
        scratch_shapes=[pltpu.VMEM((2 * tq, LANES), BF16), pltpu.VMEM((2, tq, LANES), F32),
                        pltpu.VMEM((2, tq, LANES), F32)],
        compiler_params=_params(("arbitrary", "arbitrary")),
        name="fox_prompt",
    )(q, kt, v, fk)


PAGES_PER_STEP = 8


def _suffix_kernel(lf_ref, m_ref, o_ref):
    l0, l1, l2 = _split3(lf_ref[...] * LOG2E)
    m = m_ref[...]
    o_ref[...] = _dot(l0, m) + (_dot(l1, m) + _dot(l2, m))


def _page_suffix(logf_t):
    n_phys = logf_t.shape[0]
    rows = n_phys * FOX_HEADS
    src = jnp.arange(PAGE_SIZE)
    mat = (src[:, None] >= src[None, :]).astype(BF16)
    br = 4096
    out = pl.pallas_call(
        _suffix_kernel,
        grid=(rows // br,),
        in_specs=[pl.BlockSpec((br, PAGE_SIZE), lambda i: (i, 0)),
                  pl.BlockSpec((PAGE_SIZE, PAGE_SIZE), lambda i: (0, 0))],
        out_specs=pl.BlockSpec((br, PAGE_SIZE), lambda i: (i, 0)),
        out_shape=jax.ShapeDtypeStruct((rows, PAGE_SIZE), F32),
        compiler_params=_params(("arbitrary",)),
        name="page_suffix",
    )(logf_t.reshape(rows, PAGE_SIZE), mat)
    return out.reshape(n_phys, FOX_HEADS, PAGE_SIZE)


SEQS_PER_STEP = 2


def _fox_sample_kernel(pt_ref, q_ref, kn_ref, vn_ref, fn_ref, suf_ref, *refs, n_pages):
    ns, gp = SEQS_PER_STEP, PAGES_PER_STEP
    k_refs = refs[0:ns * gp]
    v_refs = refs[ns * gp:2 * ns * gp]
    o_ref, qbd_ref, m_ref, l_ref, acc_ref, base_ref = refs[2 * ns * gp:]
    bi = pl.program_id(0)
    j = pl.program_id(1)
    nh, nq, hd = FOX_HEADS, q_ref.shape[1], FOX_HEAD_DIM
    rows = nh * nq
    seqs = range(ns)
    lane_w = lax.broadcasted_iota(jnp.int32, (rows, FOX_WIDTH), 1)
    row_w = lax.broadcasted_iota(jnp.int32, (rows, FOX_WIDTH), 0)
    own = (lane_w // hd) == (row_w // nq)

    @pl.when(j == 0)
    def _():
        for sq in seqs:
            qrep = jnp.broadcast_to(q_ref[sq][None], (nh, nq, FOX_WIDTH)).reshape(rows, FOX_WIDTH)
            qbd_ref[sq] = jnp.where(own, qrep, jnp.zeros_like(qrep))
        m_ref[...] = jnp.full(m_ref.shape, NEG_BIG, F32)
        l_ref[...] = jnp.zeros(l_ref.shape, F32)
        acc_ref[...] = jnp.zeros(acc_ref.shape, F32)
        base_ref[...] = jnp.zeros(base_ref.shape, F32)

    def update(scores, values, v_transposed):
        m_prev = [m_ref[sq] for sq in seqs]
        m_new = [jnp.maximum(m_prev[sq], jnp.max(scores[sq], axis=-1, keepdims=True)) for sq in seqs]
        p = [jnp.exp2(scores[sq] - m_new[sq]) for sq in seqs]
        pv = [(_dot_nt if v_transposed else _dot)(p[sq].astype(BF16), values[sq]) for sq in seqs]
        for sq in seqs:
            alpha = jnp.exp2(m_prev[sq] - m_new[sq])
            l_ref[sq] = alpha * l_ref[sq] + jnp.sum(p[sq], axis=-1, keepdims=True)
            acc_ref[sq] = alpha * acc_ref[sq] + pv[sq]
            m_ref[sq] = m_new[sq]

    lane_p = lax.broadcasted_iota(jnp.int32, (nh, PAGE_SIZE), 1)
    scores, values = [], []
    for sq in seqs:
        qbd = qbd_ref[sq]
        base = base_ref[sq]
        sc, va = [], []
        for g in range(gp):
            page = pt_ref[bi * ns + sq, n_pages - 1 - (j * gp + g)]
            incl = suf_ref[page]
            bias = jnp.where(lane_p < PAGE_SIZE - 1, pltpu.roll(incl, PAGE_SIZE - 1, axis=1), 0.0) + base
            base = base + incl[:, 0:1]
            kt = k_refs[sq * gp + g][...].reshape(FOX_WIDTH, PAGE_SIZE).astype(BF16)
            s = _dot(qbd, kt)
            sc.append((s.reshape(nh, nq, PAGE_SIZE) + bias[:, None, :]).reshape(rows, PAGE_SIZE))
            va.append(v_refs[sq * gp + g][...].reshape(FOX_WIDTH, PAGE_SIZE).astype(BF16))
        base_ref[sq] = base
        scores.append(jnp.concatenate(sc, axis=1))
        values.append(jnp.concatenate(va, axis=1))
    update(scores, values, True)

    @pl.when(j == pl.num_programs(1) - 1)
    def _():
        zpad = jnp.zeros((PAGE_SIZE - nq, FOX_WIDTH), BF16)
        qpos = lax.broadcasted_iota(jnp.int32, (nh, nq, PAGE_SIZE), 1)
        kpos = lax.broadcasted_iota(jnp.int32, (nh, nq, PAGE_SIZE), 2)
        sc, va = [], []
        for sq in seqs:
            kn = jnp.concatenate([kn_ref[sq], zpad], axis=0)
            s = _dot_nt(qbd_ref[sq], kn).reshape(nh, nq, PAGE_SIZE) - fn_ref[sq][:, None, :]
            sc.append(jnp.where(kpos <= qpos, s, NEG_BIG).reshape(rows, PAGE_SIZE))
            va.append(jnp.concatenate([vn_ref[sq], zpad], axis=0))
        update(sc, va, False)
        for sq in seqs:
            o = acc_ref[sq] / l_ref[sq]
            o = jnp.where(own, o, 0.0).reshape(nh, nq, FOX_WIDTH)
            o_ref[sq] = jnp.sum(o, axis=0).astype(o_ref.dtype)


def _fox_sample(page_table, q, k_new, v_new, f_new, cache_k, cache_v, suffix):
    b, t, w = q.shape
    n_pages = page_table.shape[1]
    ns, gp = SEQS_PER_STEP, PAGES_PER_STEP
    n_steps = n_pages // gp

    def page_map(sq, g):
        return lambda bi, j, pt: (pt[bi * ns + sq, n_pages - 1 - (j * gp + g)], 0, 0, 0)

    seq = lambda d1, d2: pl.BlockSpec((ns, d1, d2), lambda bi, j, pt: (bi, 0, 0))
    in_specs = [seq(t, w), seq(t, w), seq(t, w), seq(FOX_HEADS, PAGE_SIZE),
                pl.BlockSpec(suffix.shape, lambda bi, j, pt: (0, 0, 0), pipeline_mode=pl.Buffered(1))]
    kv_page = (None, FOX_HEADS, FOX_HEAD_DIM, PAGE_SIZE)
    pages = [(sq, g) for sq in range(ns) for g in range(gp)]
    in_specs += [pl.BlockSpec(kv_page, page_map(sq, g)) for sq, g in pages]
    in_specs += [pl.BlockSpec(kv_page, page_map(sq, g)) for sq, g in pages]
    rows = FOX_HEADS * t
    return pl.pallas_call(
        functools.partial(_fox_sample_kernel, n_pages=n_pages),
        grid_spec=pltpu.PrefetchScalarGridSpec(
            num_scalar_prefetch=1,
            grid=(b // ns, n_steps),
            in_specs=in_specs,
            out_specs=seq(t, w),
            scratch_shapes=[pltpu.VMEM((ns, rows, w), BF16), pltpu.VMEM((ns, rows, 1), F32),
                            pltpu.VMEM((ns, rows, 1), F32), pltpu.VMEM((ns, rows, w), F32),
                            pltpu.VMEM((ns, FOX_HEADS, PAGE_SIZE), F32)]),
        out_shape=jax.ShapeDtypeStruct((b, t, w), BF16),
        compiler_params=_params(("arbitrary", "arbitrary")),
        name="fox_sample",
    )(page_table, q, k_new, v_new, f_new, suffix, *([cache_k] * len(pages)), *([cache_v] * len(pages)))


GDN_CHUNK = 128
GDN_CHUNKS_PER_STEP = 2
_HALO = 8


def _unit_lower_inverses(strict_lowers, row, col, n_active):
    c = min(strict_lowers[0].shape[0], pl.next_power_of_2(n_active))
    eye = jnp.where(row == col, 1.0, 0.0)
    base = (row // 2) == (col // 2)
    xs = [eye - jnp.where(base, lm, 0.0) for lm in strict_lowers]
    s = 2
    while s < c:
        off_mask = ((row // (2 * s)) == (col // (2 * s))) & ((row // s) != (col // s))
        split = [_split2(x) for x in xs]
        offs = [jnp.where(off_mask, lm, 0.0).astype(BF16) for lm in strict_lowers]
        ts = [(_dot(xh, off) + _dot(xl, off)).astype(BF16) for (xh, xl), off in zip(split, offs)]
        xs = [x - (_dot(t, xh) + _dot(t, xl)) for x, t, (xh, xl) in zip(xs, ts, split)]
        s *= 2
    return xs


def _gdn_kernel(u_ref, small_ref, smallt_ref, z_ref, s0_ref, conv0_ref, convw_ref, normw_ref,
                o_ref, sout_ref, s_scr, ext_scr, *, per_seq, n_active):
    i = pl.program_id(0)
    c = GDN_CHUNK
    hd = GDN_HEAD_DIM
    heads = range(GDN_HEADS)
    subs = range(u_ref.shape[0] // c)
    probs = [(sb, h) for sb in subs for h in heads]

    if not per_seq:
        @pl.when(i == 0)
        def _():
            s_scr[...] = s0_ref[0]
            ext_scr[0:_HALO, :] = conv0_ref[0]

    ys = []
    for sb in subs:
        if per_seq:
            ext_scr[0:_HALO, :] = conv0_ref[sb]
        u = u_ref[sb * c:(sb + 1) * c, :]
        ext_scr[_HALO:_HALO + c, :] = u
        y = u * convw_ref[GDN_CONV - 1:GDN_CONV, :]
        for tap in range(GDN_CONV - 1):
            back = GDN_CONV - 1 - tap
            y = y + ext_scr[_HALO - back:_HALO - back + c, :] * convw_ref[tap:tap + 1, :]
        ext_scr[0:_HALO, :] = u[c - _HALO:c, :]
        ys.append(_silu(y))

    row = lax.broadcasted_iota(jnp.int32, (c, c), 0)
    col = lax.broadcasted_iota(jnp.int32, (c, c), 1)
    incl = row >= col
    tril = jnp.where(incl, 1.0, 0.0).astype(BF16)
    triu = jnp.where(row <= col, 1.0, 0.0).astype(BF16)
    smalls, gc_cols, gc_rows = [], [], []
    for sb in subs:
        small = small_ref[sb * c:(sb + 1) * c, :]
        c0, c1, c2 = _split3(small)
        r0, r1, r2 = _split3(smallt_ref[:, sb * c:(sb + 1) * c])
        smalls.append(small)
        gc_cols.append(_dot(tril, c0) + (_dot(tril, c1) + _dot(tril, c2)))
        gc_rows.append(_dot(r0, triu) + (_dot(r1, triu) + _dot(r2, triu)))

    def head_cols(sb, base, h):
        return ys[sb][:, base + h * hd:base + (h + 1) * hd]

    def unit(x):
        return x * lax.rsqrt(jnp.sum(x * x, axis=-1, keepdims=True) + NORM_EPS)

    q = [unit(head_cols(sb, 0, h)) * hd ** -0.5 for sb, h in probs]
    k = [unit(head_cols(sb, GDN_WIDTH, h)) for sb, h in probs]
    v = [head_cols(sb, 2 * GDN_WIDTH, h) for sb, h in probs]
    beta = [smalls[sb][:, 12 + h:13 + h] for sb, h in probs]
    gc = [gc_cols[sb][:, 8 + h:9 + h] for sb, h in probs]
    gr = [gc_rows[sb][8 + h:9 + h, :] for sb, h in probs]
    g_last = [g[c - 1:c, :] for g in gc]
    n_p = range(len(probs))
    decay = [jnp.where(incl, jnp.exp(jnp.where(incl, gc[p] - gr[p], 0.0)), 0.0) for p in n_p]
    e_gc = [jnp.exp(g) for g in gc]
    kb = [k[p] * beta[p] for p in n_p]
    k_b = [x.astype(BF16) for x in k]
    kb_s = [_split2(x) for x in kb]
    lmat = [jnp.where(row > col, (_dot_nt(kb_s[p][0], k_b[p]) + _dot_nt(kb_s[p][1], k_b[p])) * decay[p], 0.0)
            for p in n_p]
    a_intra = [jnp.where(incl, _dot_nt(q[p].astype(BF16), k_b[p]) * decay[p], 0.0).astype(BF16) for p in n_p]
    tinv = [_split2(x) for x in _unit_lower_inverses(lmat, row, col, n_active)]
    vb = [(v[p] * beta[p]).astype(BF16) for p in n_p]
    kg = [(kb[p] * e_gc[p]).astype(BF16) for p in n_p]
    uu = [_dot(tinv[p][0], vb[p]) + _dot(tinv[p][1], vb[p]) for p in n_p]
    ww = [(_dot(tinv[p][0], kg[p]) + _dot(tinv[p][1], kg[p])).astype(BF16) for p in n_p]
    q_dec = [(q[p] * e_gc[p]).astype(BF16) for p in n_p]
    k_dec = [(k[p] * jnp.exp(g_last[p] - gc[p])).astype(BF16) for p in n_p]

    for sb in subs:
        ps = [sb * GDN_HEADS + h for h in heads]
        if per_seq:
            s_scr[...] = s0_ref[sb]
        state = [s_scr[h] for h in heads]
        st_s = [_split2(x) for x in state]
        v_new = [uu[p] - (_dot(ww[p], st_s[h][0]) + _dot(ww[p], st_s[h][1])) for h, p in zip(heads, ps)]
        o = [_dot(q_dec[p], st_s[h][0]) + _dot(a_intra[p], v_new[h].astype(BF16)) for h, p in zip(heads, ps)]
        for h, p in zip(heads, ps):
            vh, vl = _split2(v_new[h])
            s_scr[h] = state[h] * jnp.exp(g_last[p]) + (_dot_tn(k_dec[p], vh) + _dot_tn(k_dec[p], vl))
        for h in heads:
            rows = slice(sb * c, (sb + 1) * c)
            sl = slice(h * hd, (h + 1) * hd)
            on = o[h] * lax.rsqrt(jnp.mean(o[h] * o[h], axis=-1, keepdims=True) + NORM_EPS) * normw_ref[...]
            o_ref[rows, sl] = (on * _silu(z_ref[rows, sl])).astype(o_ref.dtype)
        if per_seq:
            sout_ref[sb] = s_scr[...]

    if not per_seq:
        sout_ref[0] = s_scr[...]


def _gdn(u, small, small_t, z, s0, conv0, conv_w, norm_w, *, per_seq, n_active=GDN_CHUNK):
    c = GDN_CHUNK * GDN_CHUNKS_PER_STEP
    n = u.shape[0] // c
    nb = s0.shape[0]
    ns = GDN_CHUNKS_PER_STEP if per_seq else 1
    bsel = (lambda i: i) if per_seq else (lambda i: 0)
    state_block = (ns, GDN_HEADS, GDN_HEAD_DIM, GDN_HEAD_DIM)
    return pl.pallas_call(
        functools.partial(_gdn_kernel, per_seq=per_seq, n_active=n_active),
        grid=(n,),
        in_specs=[pl.BlockSpec((c, 3 * GDN_WIDTH), lambda i: (i, 0)),
                  pl.BlockSpec((c, LANES), lambda i: (i, 0)),
                  pl.BlockSpec((16, c), lambda i: (0, i)),
                  pl.BlockSpec((c, GDN_WIDTH), lambda i: (i, 0)),
                  pl.BlockSpec(state_block, lambda i: (bsel(i), 0, 0, 0)),
                  pl.BlockSpec((ns, _HALO, 3 * GDN_WIDTH), lambda i: (bsel(i), 0, 0)),
                  pl.BlockSpec((8, 3 * GDN_WIDTH), lambda i: (0, 0)),
                  pl.BlockSpec((1, GDN_HEAD_DIM), lambda i: (0, 0))],
        out_specs=[pl.BlockSpec((c, GDN_WIDTH), lambda i: (i, 0)),
                   pl.BlockSpec(state_block, lambda i: (bsel(i), 0, 0, 0))],
        out_shape=[jax.ShapeDtypeStruct((n * c, GDN_WIDTH), BF16),
                   jax.ShapeDtypeStruct((nb, GDN_HEADS, GDN_HEAD_DIM, GDN_HEAD_DIM), F32)],
        scratch_shapes=[pltpu.VMEM((GDN_HEADS, GDN_HEAD_DIM, GDN_HEAD_DIM), F32),
                        pltpu.VMEM((_HALO + GDN_CHUNK, 3 * GDN_WIDTH), F32)],
        compiler_params=_params(("arbitrary",)),
        name="gdn",
    )(u, small, small_t, z, s0, conv0, conv_w, norm_w)


_GROUP_LANE0 = N_EXPERTS


def _layer_norm(y, g, b):
    mu = jnp.mean(y, axis=-1, keepdims=True)
    d = y - mu
    var = jnp.mean(d * d, axis=-1, keepdims=True)
    return d * lax.rsqrt(var + LN_EPS) * g + b


def _pack_router(w_grp, b_grp, w_rt, b_rt):
    d = w_grp.shape[0]
    w = jnp.concatenate([w_rt, w_grp, jnp.zeros((d, LANES - N_EXPERTS - N_GROUPS), F32)], axis=1)
    b = jnp.concatenate([b_rt, b_grp, jnp.zeros((LANES - N_EXPERTS - N_GROUPS,), F32)]).reshape(1, LANES)
    wh = w.astype(BF16)
    wl = (w - wh.astype(F32)).astype(BF16)
    return wh, wl, b


def _mix_kernel(x_ref, of_ref, og_ref, gf_ref, gg_ref, wuf_ref, wug_ref, wo_ref, gate_ref, lng_ref, lnb_ref,
                shift_ref, scale_ref, wrh_ref, wrl_ref, br_ref,
                x1_ref, h2_ref, route_ref, sel_ref):
    merged = (gf_ref[...].astype(F32) * _dot(of_ref[...], wuf_ref[...])
              + gg_ref[...].astype(F32) * _dot(og_ref[...], wug_ref[...]))
    mix = _dot(merged.astype(BF16), wo_ref[...])
    x1 = _layer_norm(DEEPNORM_ALPHA * x_ref[...] + (1.0 + gate_ref[...]) * mix, lng_ref[...], lnb_ref[...])
    x1_ref[...] = x1
    h2 = x1 * (1.0 + scale_ref[...]) + shift_ref[...]
    h2_ref[...] = h2

    hh, hl = _split2(h2)
    wrh = wrh_ref[...]
    logits = _dot(hh, wrh) + (_dot(hh, wrl_ref[...]) + _dot(hl, wrh)) + br_ref[...]
    lane = lax.broadcasted_iota(jnp.int32, logits.shape, 1)
    big = jnp.int32(LANES)
    neg = jnp.float32(-jnp.inf)

    def top1(vals):
        m = jnp.max(vals, axis=-1, keepdims=True)
        idx = jnp.min(jnp.where(vals == m, lane, big), axis=-1, keepdims=True)
        return m, idx

    is_grp = (lane >= _GROUP_LANE0) & (lane < _GROUP_LANE0 + N_GROUPS)
    glog = jnp.where(is_grp, logits, neg)
    gmax, gidx = top1(glog)
    grp_p = 1.0 / jnp.sum(jnp.exp(glog - gmax), axis=-1, keepdims=True)
    grp = gidx - _GROUP_LANE0
    elog = jnp.where((lane // EXPERTS_PER_GROUP) == grp, logits, neg)
    m1, i1 = top1(elog)
    m2, i2 = top1(jnp.where(lane == i1, neg, elog))
    e21 = jnp.exp(m2 - m1)
    w1 = grp_p / (1.0 + e21)
    w2 = grp_p * e21 / (1.0 + e21)
    route = jnp.where(lane == 0, i1.astype(F32),
                      jnp.where(lane == 1, i2.astype(F32),
                                jnp.where(lane == 2, w1, jnp.where(lane == 3, w2, 0.0))))
    route_ref[...] = route
    sel_ref[...] = jnp.where((lane == i1) | (lane == i2), 1.0, 0.0).astype(BF16)


def _mix(x, o_fox, o_gdn, gf, gg, wuf, wug, wo, gate, ln_g, ln_b, shift, scale, wrh, wrl, br, *, tm):
    t, d = x.shape
    per_tok = gate.shape[0] != 1
    mod_spec = (pl.BlockSpec((tm, d), lambda i: (i, 0)) if per_tok
                else pl.BlockSpec((1, d), lambda i: (0, 0)))
    const = lambda a: pl.BlockSpec(a.shape, lambda i: (0, 0))
    tile = lambda n: pl.BlockSpec((tm, n), lambda i: (i, 0))
    return pl.pallas_call(
        _mix_kernel,
        grid=(t // tm,),
        in_specs=[tile(d), tile(FOX_WIDTH), tile(GDN_WIDTH), tile(d), tile(d), const(wuf), const(wug), const(wo),
                  mod_spec, const(ln_g), const(ln_b), mod_spec, mod_spec, const(wrh), const(wrl), const(br)],
        out_specs=[tile(d), tile(d), tile(LANES), tile(LANES)],
        out_shape=[jax.ShapeDtypeStruct((t, d), F32), jax.ShapeDtypeStruct((t, d), F32),
                   jax.ShapeDtypeStruct((t, LANES), F32), jax.ShapeDtypeStruct((t, LANES), BF16)],
        compiler_params=_params(("arbitrary",)),
        name="mix_ln_route",
    )(x, o_fox, o_gdn, gf, gg, wuf, wug, wo, gate, ln_g, ln_b, shift, scale, wrh, wrl, br)


MOE_ROWS = 512
ROW_DMA_TILE = 512


def _rank_kernel(sel_ref, route_ref, out_ref, cnt_ref, carry_ref):
    i = pl.program_id(0)
    tm = sel_ref.shape[0]

    @pl.when(i == 0)
    def _():
        carry_ref[...] = jnp.zeros_like(carry_ref)

    sel = sel_ref[...]
    row = lax.broadcasted_iota(jnp.int32, (tm, tm), 0)
    col = lax.broadcasted_iota(jnp.int32, (tm, tm), 1)
    strict = jnp.where(row > col, 1.0, 0.0).astype(BF16)
    rank = _dot(strict, sel) + carry_ref[0:1, :]
    lane = lax.broadcasted_iota(jnp.int32, rank.shape, 1)
    route = route_ref[...]
    e0 = route[:, 0:1].astype(jnp.int32)
    e1 = route[:, 1:2].astype(jnp.int32)
    r0 = jnp.sum(jnp.where(lane == e0, rank, 0.0), axis=-1, keepdims=True)
    r1 = jnp.sum(jnp.where(lane == e1, rank, 0.0), axis=-1, keepdims=True)
    out_ref[...] = jnp.where(lane == 0, r0, jnp.where(lane == 1, r1, 0.0))
    total = rank[tm - 1:tm, :] + sel[tm - 1:tm, :].astype(F32)
    carry_ref[0:1, :] = total
    cnt_ref[...] = jnp.broadcast_to(total, cnt_ref.shape)


def _rank(sel, route, *, tm):
    t = sel.shape[0]
    tile = pl.BlockSpec((tm, LANES), lambda i: (i, 0))
    return pl.pallas_call(
        _rank_kernel,
        grid=(t // tm,),
        in_specs=[tile, tile],
        out_specs=[tile, pl.BlockSpec((8, LANES), lambda i: (0, 0))],
        out_shape=[jax.ShapeDtypeStruct((t, LANES), F32), jax.ShapeDtypeStruct((8, LANES), F32)],
        scratch_shapes=[pltpu.VMEM((8, LANES), F32)],
        compiler_params=_params(("arbitrary",)),
        name="moe_rank",
    )(sel, route)


def _scatter_kernel(pos_ref, h_ref, init_ref, xs_ref, sem, *, tok0, n_tok):
    del init_ref
    i = pl.program_id(0)
    tm = h_ref.shape[0]

    def copy(r, slot):
        dst = pos_ref[slot * n_tok + tok0 + i * tm + r]
        return pltpu.make_async_copy(h_ref.at[pl.ds(r, 1), :], xs_ref.at[pl.ds(dst, 1), :], sem)

    def start(r, c):
        copy(r, 0).start(priority=0)
        copy(r, 1).start(priority=1)
        return c

    def wait(r, c):
        copy(r, 0).wait()
        copy(r, 1).wait()
        return c

    lax.fori_loop(0, tm, start, 0, unroll=8)
    lax.fori_loop(0, tm, wait, 0, unroll=8)


def _scatter_rows(pos, h, init, *, tm, tok0):
    t, d = h.shape
    n_sorted = init.shape[0]
    return pl.pallas_call(
        functools.partial(_scatter_kernel, tok0=tok0, n_tok=pos.shape[0] // 2),
        grid_spec=pltpu.PrefetchScalarGridSpec(
            num_scalar_prefetch=1,
            grid=(t // tm,),
            in_specs=[pl.BlockSpec((tm, d), lambda i, pos: (i, 0)),
                      pl.BlockSpec(memory_space=pl.ANY)],
            out_specs=pl.BlockSpec(memory_space=pl.ANY),
            scratch_shapes=[pltpu.SemaphoreType.DMA(())]),
        out_shape=jax.ShapeDtypeStruct((n_sorted, d), F32),
        input_output_aliases={2: 0},
        compiler_params=_params(("arbitrary",)),
        name="moe_scatter",
    )(pos, h, init)


def _experts_kernel(te_ref, x_ref, wg_ref, wu_ref, wd_ref, y_ref, wgb_ref, wub_ref, wdb_ref):
    i = pl.program_id(0)

    @pl.when((i == 0) | (te_ref[0, i] != te_ref[0, jnp.maximum(i - 1, 0)]))
    def _():
        wgb_ref[...] = wg_ref[...].astype(BF16)
        wub_ref[...] = wu_ref[...].astype(BF16)
        wdb_ref[...] = wd_ref[...].astype(BF16)

    @pl.when(te_ref[1, i] > 0)
    def _():
        xb = x_ref[...].astype(BF16)
        act = _silu(_dot(xb, wgb_ref[...])) * _dot(xb, wub_ref[...])
        y_ref[...] = _dot(act.astype(BF16), wdb_ref[...])

    @pl.when(te_ref[1, i] == 0)
    def _():
        y_ref[...] = jnp.zeros_like(y_ref)


def _experts(tile_info, xs, w_g, w_u, w_d):
    n_sorted, d = xs.shape
    ff = w_g.shape[2]
    n_tiles = n_sorted // MOE_ROWS
    return pl.pallas_call(
        _experts_kernel,
        grid_spec=pltpu.PrefetchScalarGridSpec(
            num_scalar_prefetch=1,
            grid=(n_tiles,),
            in_specs=[pl.BlockSpec((MOE_ROWS, d), lambda i, te: (i, 0)),
                      pl.BlockSpec((None, d, ff), lambda i, te: (te[0, i], 0, 0)),
                      pl.BlockSpec((None, d, ff), lambda i, te: (te[0, i], 0, 0)),
                      pl.BlockSpec((None, ff, d), lambda i, te: (te[0, i], 0, 0))],
            out_specs=pl.BlockSpec((MOE_ROWS, d), lambda i, te: (i, 0)),
            scratch_shapes=[pltpu.VMEM((d, ff), BF16), pltpu.VMEM((d, ff), BF16), pltpu.VMEM((ff, d), BF16)]),
        out_shape=jax.ShapeDtypeStruct((n_sorted, d), F32),
        compiler_params=_params(("arbitrary",)),
        name="moe_experts",
    )(tile_info, xs, w_g, w_u, w_d)


def _combine_kernel(pos_ref, ys_ref, x1_ref, route_ref, gate_ref, lng_ref, lnb_ref, out_ref, buf, sem,
                    *, tok0, n_tok):
    i = pl.program_id(0)
    tm = x1_ref.shape[0]

    def copy(r, slot):
        src = pos_ref[slot * n_tok + tok0 + i * tm + r]
        return pltpu.make_async_copy(ys_ref.at[pl.ds(src, 1), :], buf.at[slot, pl.ds(r, 1), :], sem)

    def start(r, c):
        copy(r, 0).start(priority=0)
        copy(r, 1).start(priority=1)
        return c

    def wait(r, c):
        copy(r, 0).wait()
        copy(r, 1).wait()
        return c

    lax.fori_loop(0, tm, start, 0, unroll=8)
    lax.fori_loop(0, tm, wait, 0, unroll=8)
    route = route_ref[...]
    ffn = route[:, 2:3] * buf[0] + route[:, 3:4] * buf[1]
    y = DEEPNORM_ALPHA * x1_ref[...] + (1.0 + gate_ref[...]) * ffn
    out_ref[...] = _layer_norm(y, lng_ref[...], lnb_ref[...])


def _combine(pos, ys, x1, route, gate, ln_g, ln_b, *, tm, tok0):
    t, d = x1.shape
    gate_spec = (pl.BlockSpec((tm, d), lambda i, pos: (i, 0)) if gate.shape[0] != 1
                 else pl.BlockSpec((1, d), lambda i, pos: (0, 0)))
    return pl.pallas_call(
        functools.partial(_combine_kernel, tok0=tok0, n_tok=pos.shape[0] // 2),
        grid_spec=pltpu.PrefetchScalarGridSpec(
            num_scalar_prefetch=1,
            grid=(t // tm,),
            in_specs=[pl.BlockSpec(memory_space=pl.ANY),
                      pl.BlockSpec((tm, d), lambda i, pos: (i, 0)),
                      pl.BlockSpec((tm, LANES), lambda i, pos: (i, 0)),
                      gate_spec,
                      pl.BlockSpec((1, d), lambda i, pos: (0, 0)),
                      pl.BlockSpec((1, d), lambda i, pos: (0, 0))],
            out_specs=pl.BlockSpec((tm, d), lambda i, pos: (i, 0)),
            scratch_shapes=[pltpu.VMEM((2, tm, d), F32), pltpu.SemaphoreType.DMA(())]),
        out_shape=jax.ShapeDtypeStruct((t, d), F32),
        compiler_params=_params(("arbitrary",)),
        name="moe_combine_ln",
    )(pos, ys, x1, route, gate, ln_g, ln_b)


def _moe(groups, w_g, w_u, w_d, ln_g, ln_b, *, tm):
    route = jnp.concatenate([g[2] for g in groups], axis=0)
    sel = jnp.concatenate([g[3] for g in groups], axis=0)
    n = route.shape[0]
    ranks, counts = _rank(sel, route, tm=tm)
    counts = counts[0, :N_EXPERTS].astype(jnp.int32)
    padded = ((counts + MOE_ROWS - 1) // MOE_ROWS) * MOE_ROWS
    ends = jnp.cumsum(padded)
    offsets = ends - padded
    n_tiles = (2 * n) // MOE_ROWS + N_EXPERTS
    tile_start = jnp.arange(n_tiles, dtype=jnp.int32) * MOE_ROWS
    tile_expert = jnp.sum((tile_start[:, None] >= ends[None, :]).astype(jnp.int32), axis=1)
    tile_expert = jnp.minimum(tile_expert, N_EXPERTS - 1)
    tile_used = (tile_start < ends[-1]).astype(jnp.int32)
    tile_info = jnp.stack([tile_expert, tile_used])
    e = jnp.transpose(route[:, 0:2]).astype(jnp.int32)
    r = jnp.transpose(ranks[:, 0:2])
    one_hot = (e[:, :, None] == jnp.arange(N_EXPERTS, dtype=jnp.int32)).astype(F32)
    seg = jnp.einsum('sne,e->sn', one_hot, offsets.astype(F32), precision=lax.Precision.HIGHEST)
    pos = (seg + r).astype(jnp.int32).reshape(-1)
    xs = jnp.zeros((n_tiles * MOE_ROWS, h2_dim(groups)), F32)
    tok0 = 0
    for h2, _, _, _, _ in groups:
        xs = _scatter_rows(pos, h2, xs, tm=_dma_tile(h2.shape[0]), tok0=tok0)
        tok0 += h2.shape[0]
    ys = _experts(tile_info, xs, w_g, w_u, w_d)
    outs = []
    tok0 = 0
    for _, x1, rt, _, gate in groups:
        outs.append(_combine(pos, ys, x1, rt, gate, ln_g, ln_b, tm=_dma_tile(x1.shape[0]), tok0=tok0))
        tok0 += x1.shape[0]
    return outs


def h2_dim(groups):
    return groups[0][0].shape[1]


def _dma_tile(n_rows):
    return ROW_DMA_TILE if n_rows % ROW_DMA_TILE == 0 else n_rows


TOKEN_TILE = 256
ATTN_TILE = 512
QUERY_TILE = 1024


def kernel(x_prompt, x_sample, cache_fox_k, cache_fox_v, cache_fox_logf, state_gdn, state_gdn_conv, page_table,
           c_prompt, c_sample, w_ada_mix, b_ada_mix, w_in, b_forget, gdn_conv_w, gdn_a_log, gdn_dt_bias,
           gdn_norm_w, w_up_fox, w_up_gdn, w_out, ln1_g, ln1_b, w_ada_ffn, b_ada_ffn, w_group_router,
           b_group_router, w_expert_router, b_expert_router, w_expert_gate, w_expert_up, w_expert_down,
           ln2_g, ln2_b):
    assert w_in.shape[0] == 1, "single-layer step"
    bp, tp, d = x_prompt.shape
    bs, ts, _ = x_sample.shape
    assert bp == 1
    tm = TOKEN_TILE
    row = lambda a: a.reshape(1, -1)

    n_c = bp + bs
    c_all = jnp.concatenate([c_prompt, c_sample, jnp.zeros((-n_c % 8, d), F32)], axis=0)
    mod_mix = _adaln(c_all, w_ada_mix[0], b_ada_mix[0])
    mod_ffn = _adaln(c_all, w_ada_ffn[0], b_ada_ffn[0])

    def split_mod(mod, lo, hi, rep):
        m = mod[lo:hi]
        if rep > 1:
            m = jnp.repeat(m, rep, axis=0)
        return m[:, 0:d], m[:, d:2 * d], m[:, 2 * d:3 * d]

    wide, wsh, wsl = _pack_w_in(jnp.transpose(w_in[0]))
    prow, pcol = _param_rows(b_forget[0], gdn_a_log[0], gdn_dt_bias[0])
    conv_w = jnp.pad(gdn_conv_w[0], ((0, 8 - GDN_CONV), (0, 0)))
    norm_w = row(gdn_norm_w[0])
    wuf, wug, wo = w_up_fox[0].astype(BF16), w_up_gdn[0].astype(BF16), w_out[0].astype(BF16)
    wrh, wrl, br = _pack_router(w_group_router[0], b_group_router[0], w_expert_router[0], b_expert_router[0])
    ln1 = (row(ln1_g[0]), row(ln1_b[0]))
    heads = (FOX_HEADS, FOX_HEAD_DIM)

    shift1, scale1, gate1 = split_mod(mod_mix, 0, 1, 1)
    shift2, scale2, gate2_p = split_mod(mod_ffn, 0, 1, 1)
    xp = x_prompt[0]
    pr = _project(xp, shift1, scale1, wide, wsh, wsl, prow, pcol, tm=ATTN_TILE, seg_len=None, kv_transposed=True)
    o_fox = _fox_prompt(pr['q'], pr['ktb'], pr['vb'], pr['ft'][0:FOX_HEADS], tq=QUERY_TILE)
    o_gdn, state_p = _gdn(pr['gqkv'], pr['small'], pr['small_t'], pr['gz'],
                          jnp.zeros((1, GDN_HEADS, GDN_HEAD_DIM, GDN_HEAD_DIM), F32),
                          jnp.zeros((1, _HALO, 3 * GDN_WIDTH), F32), conv_w, norm_w, per_seq=False)
    x1_p, h2_p, route_p, sel_p = _mix(xp, o_fox, o_gdn, pr['gf'], pr['gg'], wuf, wug, wo, gate1, *ln1,
                                      shift2, scale2, wrh, wrl, br, tm=ATTN_TILE)
    head_major = lambda a: jnp.transpose(a.reshape(*heads, tp), (2, 0, 1))
    prompt_rows = (head_major(pr['ktf']), head_major(pr['vtf']), jnp.transpose(pr['small_t'][0:FOX_HEADS]),
                   state_p, pr['gqkv'][tp - (GDN_CONV - 1):])

    n_s = bs * ts
    shift1, scale1, gate1 = split_mod(mod_mix, 1, 1 + bs, ts)
    shift2, scale2, gate2_s = split_mod(mod_ffn, 1, 1 + bs, ts)
    xs = x_sample.reshape(n_s, d)
    sr = _project(xs, shift1, scale1, wide, wsh, wsl, prow, pcol, tm=n_s, seg_len=ts, kv_transposed=False)
    suffix = _page_suffix(jnp.transpose(cache_fox_logf[0], (0, 2, 1)))
    lanes_per_seq = lambda a: a.reshape(a.shape[0], bs, ts)
    f_new = jnp.pad(jnp.transpose(lanes_per_seq(sr['ft'][0:FOX_HEADS]), (1, 0, 2)),
                    ((0, 0), (0, 0), (0, PAGE_SIZE - ts)))
    seq3 = lambda a: a.reshape(bs, ts, a.shape[-1])
    o_fox = _fox_sample(page_table, seq3(sr['q']), seq3(sr['kb']), seq3(sr['vb']), f_new,
                        jnp.transpose(cache_fox_k[0], (0, 2, 3, 1)), jnp.transpose(cache_fox_v[0], (0, 2, 3, 1)),
                        suffix).reshape(n_s, FOX_WIDTH)
    chunk = lambda a: jnp.pad(seq3(a), ((0, 0), (0, GDN_CHUNK - ts), (0, 0))).reshape(bs * GDN_CHUNK, a.shape[-1])
    small_t_c = jnp.pad(lanes_per_seq(sr['small_t']), ((0, 0), (0, 0), (0, GDN_CHUNK - ts))).reshape(N_SMALL, -1)
    conv0 = jnp.pad(state_gdn_conv[0], ((0, 0), (_HALO - (GDN_CONV - 1), 0), (0, 0)))
    o_gdn, state_s = _gdn(chunk(sr['gqkv']), chunk(sr['small']), small_t_c, chunk(sr['gz']), state_gdn[0], conv0,
                          conv_w, norm_w, per_seq=True, n_active=ts)
    o_gdn = o_gdn.reshape(bs, GDN_CHUNK, GDN_WIDTH)[:, 0:ts].reshape(n_s, GDN_WIDTH)
    x1_s, h2_s, route_s, sel_s = _mix(xs, o_fox, o_gdn, sr['gf'], sr['gg'], wuf, wug, wo, gate1, *ln1,
                                      shift2, scale2, wrh, wrl, br, tm=n_s)
    sample_rows = (sr['kf'], sr['vf'], sr['small'][:, 0:FOX_HEADS], state_s,
                   seq3(sr['gqkv'])[:, ts - (GDN_CONV - 1):])

    y_p, y_s = _moe([(h2_p, x1_p, route_p, sel_p, gate2_p), (h2_s, x1_s, route_s, sel_s, gate2_s)],
                    w_expert_gate[0], w_expert_up[0], w_expert_down[0], row(ln2_g[0]), row(ln2_b[0]), tm=tm)

    kp, vp, lfp, sp, cvp = prompt_rows
    ks, vs, lfs, ss, cvs = sample_rows
    return (y_p.reshape(1, tp, d), y_s.reshape(bs, ts, d),
            kp.reshape(1, 1, tp, *heads), vp.reshape(1, 1, tp, *heads), lfp.reshape(1, 1, tp, FOX_HEADS),
            sp.reshape(1, 1, GDN_HEADS, GDN_HEAD_DIM, GDN_HEAD_DIM), cvp.reshape(1, 1, GDN_CONV - 1, 3 * GDN_WIDTH),
            ks.reshape(1, bs, ts, *heads), vs.reshape(1, bs, ts, *heads), lfs.reshape(1, bs, ts, FOX_HEADS),
            ss.reshape(1, bs, GDN_HEADS, GDN_HEAD_DIM, GDN_HEAD_DIM),
            cvs.reshape(1, bs, GDN_CONV - 1, 3 * GDN_WIDTH))
```
